```python
import math
import jax, jax.numpy as jnp
from jax import lax
import numpy as np

D_MODEL = 4096
BATCH = 2
SEQ = 8192
DEPTH = 4

N_MIXERS = 4
NORM_EPS = 1e-6

MLSTM_HEADS = 8
MLSTM_DQK = D_MODEL // (2 * MLSTM_HEADS)
MLSTM_DV = D_MODEL // MLSTM_HEADS
MLSTM_CHUNK = 64

DIFF_DH = 128
DIFF_HEADS = D_MODEL // (2 * DIFF_DH)
Q_BLOCK = 128

SSM_D_INNER = 2 * D_MODEL
SSM_HEADDIM = 64
SSM_HEADS = SSM_D_INNER // SSM_HEADDIM
SSM_STATE = 128
SSM_GROUPS = 8
SSM_CONV = 4
SSM_CHUNK = 64
DT_MIN = 1e-3
DT_MAX = 1e-1

HGRN_DK = 128
HGRN_HEADS = D_MODEL // HGRN_DK
HGRN_DV = D_MODEL // HGRN_HEADS
HGRN_CHUNK = 64

FFN_DIM = ((8 * D_MODEL // 3 + 255) // 256) * 256
FFN_CONV = 3

kernel_name = "hybrid_interleaved_mlstm_diffattn_ssd_hgrn2"


def rms_norm(x, g):
    xf = x.astype(jnp.float32)
    y = xf * lax.rsqrt(jnp.mean(xf * xf, axis=-1, keepdims=True) + NORM_EPS)
    return (y * g.astype(jnp.float32)).astype(x.dtype)


def causal_dwconv(x, w, b):
    width, ch = w.shape
    y = lax.conv_general_dilated(x, w[:, None, :].astype(x.dtype), window_strides=(1,),
                                 padding=[(width - 1, 0)],
                                 dimension_numbers=('NWC', 'WIO', 'NWC'),
                                 feature_group_count=ch)
    return y + b


def to_chunks(t, chunk, heads, dim):
    b, s, _ = t.shape
    return t.reshape(b, s // chunk, chunk, heads, dim).transpose(1, 0, 3, 2, 4).astype(jnp.float32)


def from_chunks(t):
    nc, b, h, l, d = t.shape
    return t.transpose(1, 0, 3, 2, 4).reshape(b, nc * l, h, d)


def mlstm_mixer(h, w_in, b_gates, w_norm, w_out):
    H, DK, DV, L = MLSTM_HEADS, MLSTM_DQK, MLSTM_DV, MLSTM_CHUNK
    bsz, s, _ = h.shape
    nc = s // L
    q, k, v, o, gates = jnp.split(h @ w_in, [H * DK, 2 * H * DK, 2 * H * DK + H * DV,
                                            2 * H * DK + 2 * H * DV], axis=-1)
    gates = (gates + b_gates).astype(jnp.float32)
    ic = gates[..., :H].reshape(bsz, nc, L, H).transpose(1, 0, 3, 2)
    fc = jax.nn.log_sigmoid(gates[..., H:]).reshape(bsz, nc, L, H).transpose(1, 0, 3, 2)
    qc = to_chunks(q, L, H, DK) * (DK ** -0.5)
    kc = to_chunks(k, L, H, DK)
    vc = to_chunks(v, L, H, DV)
    causal = jnp.tril(jnp.ones((L, L), dtype=bool))

    def step(carry, inp):
        C, n, m = carry
        q_, k_, v_, i_, lf = inp
        b = jnp.cumsum(lf, axis=-1)
        a = b + m[..., None]
        d = jnp.where(causal, b[..., :, None] - b[..., None, :] + i_[..., None, :], -jnp.inf)
        m_t = jnp.maximum(a, jnp.max(d, axis=-1))
        w_intra = jnp.exp(d - m_t[..., None])
        w_inter = jnp.exp(a - m_t)
        sc = jnp.einsum('bhtk,bhsk->bhts', q_, k_) * w_intra
        num = jnp.einsum('bhts,bhsv->bhtv', sc, v_) + w_inter[..., None] * jnp.einsum('bhtk,bhkv->bhtv', q_, C)
        den = jnp.sum(sc, axis=-1) + w_inter * jnp.einsum('bhtk,bhk->bht', q_, n)
        out = num / jnp.maximum(jnp.abs(den), jnp.exp(-m_t))[..., None]
        b_last = b[..., -1]
        g = b_last[..., None] - b + i_
        m_new = jnp.maximum(b_last + m, jnp.max(g, axis=-1))
        ws = jnp.exp(g - m_new[..., None])
        decay = jnp.exp(b_last + m - m_new)
        C_new = decay[..., None, None] * C + jnp.einsum('bhsk,bhsv->bhkv', ws[..., None] * k_, v_)
        n_new = decay[..., None] * n + jnp.einsum('bhs,bhsk->bhk', ws, k_)
        return (C_new, n_new, m_new), out

    init = (jnp.zeros((bsz, H, DK, DV), jnp.float32), jnp.zeros((bsz, H, DK), jnp.float32),
            jnp.zeros((bsz, H), jnp.float32))
    _, hc = lax.scan(step, init, (qc, kc, vc, ic, fc))
    hh = rms_norm(from_chunks(hc), w_norm.reshape(H, DV))
    hh = jax.nn.sigmoid(o.astype(jnp.float32)) * hh.reshape(bsz, s, H * DV)
    return hh.astype(h.dtype) @ w_out


def diff_attn_mixer(h, w_in, lam_q1, lam_k1, lam_q2, lam_k2, w_subln, w_out, lambda_init):
    H, DH = DIFF_HEADS, DIFF_DH
    bsz, s, _ = h.shape
    nb = s // Q_BLOCK
    q, k, v = jnp.split(h @ w_in, 3, axis=-1)
    q = q.reshape(bsz, s, H, 2, DH).transpose(0, 2, 3, 1, 4)
    k = k.reshape(bsz, s, H, 2, DH).transpose(0, 2, 3, 1, 4)
    v = v.reshape(bsz, s, H, 2 * DH).transpose(0, 2, 1, 3)
    lam = (jnp.exp(jnp.sum((lam_q1 * lam_k1).astype(jnp.float32)))
           - jnp.exp(jnp.sum((lam_q2 * lam_k2).astype(jnp.float32))) + lambda_init)
    slopes = jnp.exp2(-8.0 * jnp.arange(1, H + 1, dtype=jnp.float32) / H)
    qb = q.reshape(bsz, H, 2, nb, Q_BLOCK, DH).transpose(3, 0, 1, 2, 4, 5)
    pos_k = jnp.arange(s)
    scale = DH ** -0.5

    def block(args):
        qi, idx = args
        pos_q = idx * Q_BLOCK + jnp.arange(Q_BLOCK)
        dist = (pos_q[:, None] - pos_k[None, :]).astype(jnp.float32)
        bias = jnp.where(dist >= 0, -slopes[:, None, None] * dist, -jnp.inf)
        sc = jnp.einsum('bhcqd,bhcsd->bhcqs', qi, k).astype(jnp.float32) * scale + bias[None, :, None]
        p = jax.nn.softmax(sc, axis=-1)
        attn = p[:, :, 0] - lam * p[:, :, 1]
        return jnp.einsum('bhqs,bhsv->bhqv', attn.astype(v.dtype), v)

    o = lax.map(block, (qb, jnp.arange(nb)))
    o = o.transpose(1, 0, 3, 2, 4).reshape(bsz, s, H, 2 * DH)
    o = rms_norm(o, w_subln).astype(jnp.float32) * (1.0 - lambda_init)
    return o.reshape(bsz, s, H * 2 * DH).astype(h.dtype) @ w_out


def mamba2_mixer(h, w_in, conv_w, conv_b, dt_bias, a_log, d_skip, w_norm, w_out):
    DI, NH, P, N, G, L = SSM_D_INNER, SSM_HEADS, SSM_HEADDIM, SSM_STATE, SSM_GROUPS, SSM_CHUNK
    HPG = NH // G
    GN = G * N
    bsz, s, _ = h.shape
    nc = s // L
    z, xbc, dt = jnp.split(h @ w_in, [DI, 2 * DI + 2 * GN], axis=-1)
    xbc = jax.nn.silu(causal_dwconv(xbc, conv_w, conv_b))
    xs, Bm, Cm = jnp.split(xbc, [DI, DI + GN], axis=-1)
    dt = jax.nn.softplus((dt + dt_bias).astype(jnp.float32))
    A = -jnp.exp(a_log.astype(jnp.float32)).reshape(G, HPG)
    xc = xs.reshape(bsz, nc, L, G, HPG, P).transpose(1, 0, 2, 3, 4, 5).astype(jnp.float32)
    Bc = Bm.reshape(bsz, nc, L, G, N).transpose(1, 0, 2, 3, 4).astype(jnp.float32)
    Cc = Cm.reshape(bsz, nc, L, G, N).transpose(1, 0, 2, 3, 4).astype(jnp.float32)
    dtc = dt.reshape(bsz, nc, L, G, HPG).transpose(1, 0, 2, 3, 4)
    causal = jnp.tril(jnp.ones((L, L), dtype=bool))[None, :, :, None, None]

    def step(state, inp):
        x_, B_, C_, dt_ = inp
        cum = jnp.cumsum(dt_ * A, axis=1)
        seg = cum[:, :, None] - cum[:, None, :]
        decay = jnp.exp(jnp.where(causal, seg, -jnp.inf))
        cb = jnp.einsum('btgn,bsgn->btsg', C_, B_)
        w = cb[..., None] * decay * dt_[:, None]
        y_intra = jnp.einsum('btsgh,bsghp->btghp', w, x_)
        y_inter = jnp.einsum('btgn,bghpn->btghp', C_, state) * jnp.exp(cum)[..., None]
        w_last = jnp.exp(cum[:, -1:] - cum) * dt_
        state_new = (jnp.exp(cum[:, -1])[..., None, None] * state
                     + jnp.einsum('bsgn,bsghp->bghpn', B_, w_last[..., None] * x_))
        return state_new, y_intra + y_inter

    init = jnp.zeros((bsz, G, HPG, P, N), jnp.float32)
    _, yc = lax.scan(step, init, (xc, Bc, Cc, dtc))
    y = yc.transpose(1, 0, 2, 3, 4, 5).reshape(bsz, s, NH, P)
    y = y + xs.reshape(bsz, s, NH, P).astype(jnp.float32) * d_skip.astype(jnp.float32)[:, None]
    y = y.reshape(bsz, s, DI) * jax.nn.silu(z.astype(jnp.float32))
    y = rms_norm(y.reshape(bsz, s, G, DI // G), w_norm.reshape(G, DI // G))
    return y.reshape(bsz, s, DI).astype(h.dtype) @ w_out


def hgrn2_mixer(h, w_in, lower_bound, w_gnorm, w_out):
    H, DK, DV, L = HGRN_HEADS, HGRN_DK, HGRN_DV, HGRN_CHUNK
    bsz, s, _ = h.shape
    q, f, i, g = jnp.split(h @ w_in, 4, axis=-1)
    q = jax.nn.silu(q)
    f = f.astype(jnp.float32)
    lb = lower_bound.astype(jnp.float32)
    key = (1.0 - lb) * jax.nn.sigmoid(-f)
    log_f = jnp.logaddexp(jnp.log(lb), jnp.log1p(-lb) + jax.nn.log_sigmoid(f))
    qc = to_chunks(q, L, H, DK)
    kc = to_chunks(key, L, H, DK)
    vc = to_chunks(i, L, H, DV)
    fc = to_chunks(log_f, L, H, DK)
    causal = jnp.tril(jnp.ones((L, L), dtype=bool))[:, :, None]

    def step(S, inp):
        q_, k_, v_, lf = inp
        b = jnp.cumsum(lf, axis=2)
        rel = b[:, :, :, None] - b[:, :, None, :]
        dec = jnp.exp(jnp.where(causal, rel, -jnp.inf))
        attn = jnp.sum(q_[:, :, :, None] * dec * k_[:, :, None], axis=-1)
        o = (jnp.einsum('bhts,bhsv->bhtv', attn, v_)
             + jnp.einsum('bhtk,bhkv->bhtv', q_ * jnp.exp(b), S))
        b_last = b[:, :, -1]
        S_new = (jnp.exp(b_last)[..., None] * S
                 + jnp.einsum('bhsk,bhsv->bhkv', k_ * jnp.exp(b_last[:, :, None] - b), v_))
        return S_new, o

    _, oc = lax.scan(step, jnp.zeros((bsz, H, DK, DV), jnp.float32), (qc, kc, vc, fc))
    o = rms_norm(from_chunks(oc), w_gnorm).astype(jnp.float32)
    o = o * jax.nn.silu(g.astype(jnp.float32)).reshape(bsz, s, H, DV)
    return o.reshape(bsz, s, H * DV).astype(h.dtype) @ w_out


def conv_ffn(h, w_up, conv_w, conv_b, w_down):
    u = causal_dwconv(h @ w_up, conv_w, conv_b)
    gate, val = jnp.split(u, 2, axis=-1)
    return (jax.nn.silu(gate) * val) @ w_down


def setup_inputs(seed: int = 0) -> dict:
    key = jax.random.key(seed)
    keys = iter(jax.random.split(key, 64))
    f32 = jnp.float32
    D = D_MODEL

    def nrm(shape, scale):
        return jax.random.normal(next(keys), shape, f32) * scale

    def gain(shape):
        return 1.0 + nrm(shape, 0.02)

    nA, nB, nC, nD = [len(range(t, DEPTH, N_MIXERS)) for t in range(N_MIXERS)]
    x = nrm((BATCH, SEQ, D), 1.0)
    norm_mix = gain((DEPTH, D))
    norm_ffn = gain((DEPTH, D))
    norm_final = gain((D,))
    m_cols = 2 * MLSTM_HEADS * MLSTM_DQK + 2 * MLSTM_HEADS * MLSTM_DV + 2 * MLSTM_HEADS
    mlstm_w_in = nrm((nA, D, m_cols), D ** -0.5)
    i_bias = nrm((nA, MLSTM_HEADS), 0.1)
    f_bias = jnp.linspace(3.0, 6.0, MLSTM_HEADS, dtype=f32)[None] + nrm((nA, MLSTM_HEADS), 0.1)
    mlstm_b_gates = jnp.concatenate([i_bias, f_bias], axis=-1)
    mlstm_w_norm = gain((nA, MLSTM_HEADS * MLSTM_DV))
    mlstm_w_out = nrm((nA, MLSTM_HEADS * MLSTM_DV, D), (MLSTM_HEADS * MLSTM_DV) ** -0.5)
    diff_w_in = nrm((nB, D, 3 * D), D ** -0.5)
    diff_lam_q1 = nrm((nB, DIFF_DH), 0.1)
    diff_lam_k1 = nrm((nB, DIFF_DH), 0.1)
    diff_lam_q2 = nrm((nB, DIFF_DH), 0.1)
    diff_lam_k2 = nrm((nB, DIFF_DH), 0.1)
    diff_w_subln = gain((nB, 2 * DIFF_DH))
    diff_w_out = nrm((nB, D, D), D ** -0.5)
    s_cols = 2 * SSM_D_INNER + 2 * SSM_GROUPS * SSM_STATE + SSM_HEADS
    conv_ch = SSM_D_INNER + 2 * SSM_GROUPS * SSM_STATE
    ssm_w_in = nrm((nC, D, s_cols), D ** -0.5)
    ssm_conv_w = nrm((nC, SSM_CONV, conv_ch), SSM_CONV ** -0.5)
    ssm_conv_b = nrm((nC, conv_ch), 0.02)
    u = jax.random.uniform(next(keys), (nC, SSM_HEADS), f32)
    dt0 = jnp.exp(u * (math.log(DT_MAX) - math.log(DT_MIN)) + math.log(DT_MIN))
    ssm_dt_bias = dt0 + jnp.log(-jnp.expm1(-dt0))
    ssm_a_log = jnp.log(jax.random.uniform(next(keys), (nC, SSM_HEADS), f32, 1.0, 16.0))
    ssm_d = gain((nC, SSM_HEADS))
    ssm_w_norm = gain((nC, SSM_D_INNER))
    ssm_w_out = nrm((nC, SSM_D_INNER, D), SSM_D_INNER ** -0.5)
    hgrn_w_in = nrm((nD, D, 4 * D), D ** -0.5)
    hgrn_lb = nrm((DEPTH, D), 0.1)
    hgrn_w_gnorm = gain((nD, HGRN_DV))
    hgrn_w_out = nrm((nD, D, D), D ** -0.5)
    ffn_w_up = nrm((DEPTH, D, 2 * FFN_DIM), D ** -0.5)
    ffn_conv_w = nrm((DEPTH, FFN_CONV, 2 * FFN_DIM), FFN_CONV ** -0.5)
    ffn_conv_b = nrm((DEPTH, 2 * FFN_DIM), 0.02)
    ffn_w_down = nrm((DEPTH, FFN_DIM, D), FFN_DIM ** -0.5)
    return {"x": x, "norm_mix": norm_mix, "norm_ffn": norm_ffn, "norm_final": norm_final,
            "mlstm_w_in": mlstm_w_in, "mlstm_b_gates": mlstm_b_gates, "mlstm_w_norm": mlstm_w_norm,
            "mlstm_w_out": mlstm_w_out,
            "diff_w_in": diff_w_in, "diff_lam_q1": diff_lam_q1, "diff_lam_k1": diff_lam_k1,
            "diff_lam_q2": diff_lam_q2, "diff_lam_k2": diff_lam_k2, "diff_w_subln": diff_w_subln,
            "diff_w_out": diff_w_out,
            "ssm_w_in": ssm_w_in, "ssm_conv_w": ssm_conv_w, "ssm_conv_b": ssm_conv_b,
            "ssm_dt_bias": ssm_dt_bias, "ssm_a_log": ssm_a_log, "ssm_d": ssm_d,
            "ssm_w_norm": ssm_w_norm, "ssm_w_out": ssm_w_out,
            "hgrn_w_in": hgrn_w_in, "hgrn_lb": hgrn_lb, "hgrn_w_gnorm": hgrn_w_gnorm,
            "hgrn_w_out": hgrn_w_out,
            "ffn_w_up": ffn_w_up, "ffn_conv_w": ffn_conv_w, "ffn_conv_b": ffn_conv_b,
            "ffn_w_down": ffn_w_down}


def reference(x, norm_mix, norm_ffn, norm_final,
              mlstm_w_in, mlstm_b_gates, mlstm_w_norm, mlstm_w_out,
              diff_w_in, diff_lam_q1, diff_lam_k1, diff_lam_q2, diff_lam_k2, diff_w_subln, diff_w_out,
              ssm_w_in, ssm_conv_w, ssm_conv_b, ssm_dt_bias, ssm_a_log, ssm_d, ssm_w_norm, ssm_w_out,
              hgrn_w_in, hgrn_lb, hgrn_w_gnorm, hgrn_w_out,
              ffn_w_up, ffn_conv_w, ffn_conv_b, ffn_w_down):
    lb_soft = jax.nn.softmax(hgrn_lb.astype(jnp.float32), axis=0)
    lower_bounds = jnp.cumsum(lb_soft, axis=0) - lb_soft[0]
    for layer in range(DEPTH):
        kind, j = layer % N_MIXERS, layer // N_MIXERS
        hn = rms_norm(x, norm_mix[layer])
        if kind == 0:
            y = mlstm_mixer(hn, mlstm_w_in[j], mlstm_b_gates[j], mlstm_w_norm[j], mlstm_w_out[j])
        elif kind == 1:
            lambda_init = 0.8 - 0.6 * math.exp(-0.3 * layer)
            y = diff_attn_mixer(hn, diff_w_in[j], diff_lam_q1[j], diff_lam_k1[j], diff_lam_q2[j],
                                diff_lam_k2[j], diff_w_subln[j], diff_w_out[j], lambda_init)
        elif kind == 2:
            y = mamba2_mixer(hn, ssm_w_in[j], ssm_conv_w[j], ssm_conv_b[j], ssm_dt_bias[j],
                             ssm_a_log[j], ssm_d[j], ssm_w_norm[j], ssm_w_out[j])
        else:
            y = hgrn2_mixer(hn, hgrn_w_in[j], lower_bounds[layer], hgrn_w_gnorm[j], hgrn_w_out[j])
        x = x + y
        x = x + conv_ffn(rms_norm(x, norm_ffn[layer]), ffn_w_up[layer], ffn_conv_w[layer],
                         ffn_conv_b[layer], ffn_w_down[layer])
    return rms_norm(x, norm_final)
```

```python
import functools
import math

import jax
import jax.numpy as jnp
from jax import lax
from jax.experimental import pallas as pl
from jax.experimental.pallas import tpu as pltpu

NORM_EPS = 1e-6
N_MIXERS = 4
MLSTM_CHUNK = 64
DIFF_DH = 128
Q_BLOCK = 128
SSM_HEADDIM = 64
SSM_STATE = 128
SSM_GROUPS = 8
SSM_CHUNK = 64
HGRN_DK = 128
HGRN_CHUNK = 64

V7X_VMEM_BYTES = 64 * 1024 * 1024
VMEM_LIMIT_BYTES = 56 * 1024 * 1024
LANES = 128
SUBLANES = 8

F32 = jnp.float32
BF16 = jnp.bfloat16


def _cparams(*sem):
    return pltpu.CompilerParams(dimension_semantics=sem, vmem_limit_bytes=VMEM_LIMIT_BYTES)


def _mm_kernel(x_ref, w_ref, *rest, has_res, nk):
    if has_res:
        r_ref, o_ref = rest
    else:
        (o_ref,) = rest
    part = jnp.dot(x_ref[...], w_ref[...], preferred_element_type=F32)
    if nk == 1:
        if has_res:
            part = part + r_ref[...]
        o_ref[...] = part.astype(o_ref.dtype)
    else:
        k = pl.program_id(2)

        @pl.when(k == 0)
        def _():
            if has_res:
                o_ref[...] = r_ref[...] + part
            else:
                o_ref[...] = part

        @pl.when(k > 0)
        def _():
            o_ref[...] += part


def _pick_tile(n, target, quantum):
    best = None
    t = quantum
    while t <= min(n, target):
        if n % t == 0:
            best = t
        t += quantum
    assert best is not None, (n, target, quantum)
    return best


def matmul(x, w, residual=None, out_dtype=F32, tm=1024, tn=512, tk=4096):
    m, k = x.shape
    k2, n = w.shape
    assert k == k2
    tm = _pick_tile(m, tm, SUBLANES)
    tn = _pick_tile(n, tn, LANES)
    tk = k if k <= tk else _pick_tile(k, max(tk, 5504), LANES)
    nk = k // tk
    if nk > 1:
        assert out_dtype == F32
    has_res = residual is not None
    in_specs = [pl.BlockSpec((tm, tk), lambda i, j, kk: (i, kk)),
                pl.BlockSpec((tk, tn), lambda i, j, kk: (kk, j))]
    args = [x, w]
    if has_res:
        in_specs.append(pl.BlockSpec((tm, tn), lambda i, j, kk: (i, j)))
        args.append(residual)
    return pl.pallas_call(
        functools.partial(_mm_kernel, has_res=has_res, nk=nk),
        grid=(m // tm, n // tn, nk),
        in_specs=in_specs,
        out_specs=pl.BlockSpec((tm, tn), lambda i, j, kk: (i, j)),
        out_shape=jax.ShapeDtypeStruct((m, n), out_dtype),
        compiler_params=_cparams("parallel", "parallel", "arbitrary"),
    )(*args)


def _rmsnorm_kernel(x_ref, g_ref, o_ref):
    x = x_ref[...].astype(F32)
    ms = jnp.mean(x * x, axis=-1, keepdims=True)
    o_ref[...] = (x * lax.rsqrt(ms + NORM_EPS) * g_ref[...]).astype(o_ref.dtype)


def rmsnorm(x, g, out_dtype=BF16, tm=512):
    m, d = x.shape
    tm = _pick_tile(m, tm, SUBLANES)
    return pl.pallas_call(
        _rmsnorm_kernel,
        grid=(m // tm,),
        in_specs=[pl.BlockSpec((tm, d), lambda i: (i, 0)),
                  pl.BlockSpec((1, d), lambda i: (0, 0))],
        out_specs=pl.BlockSpec((tm, d), lambda i: (i, 0)),
        out_shape=jax.ShapeDtypeStruct((m, d), out_dtype),
        compiler_params=_cparams("parallel"),
    )(x, g.reshape(1, d).astype(F32))


def _shift_rows(cur, prev_tail, shift):
    rolled = pltpu.roll(cur, shift, axis=0)
    tail = pltpu.roll(prev_tail, shift, axis=0)
    row = lax.broadcasted_iota(jnp.int32, (SUBLANES, cur.shape[1]), 0)
    head = jnp.where(row < shift, tail, rolled[:SUBLANES])
    return jnp.concatenate([head, rolled[SUBLANES:]], axis=0)


def _ffn_act_kernel(g_ref, v_ref, wg_ref, wv_ref, bg_ref, bv_ref, o_ref, gt_ref, vt_ref):
    s = pl.program_id(2)

    @pl.when(s == 0)
    def _():
        gt_ref[...] = jnp.zeros_like(gt_ref)
        vt_ref[...] = jnp.zeros_like(vt_ref)

    def conv(u_ref, t_ref, w_ref, b_ref):
        u = u_ref[...].astype(F32)
        tail = t_ref[...]
        y = (w_ref[2:3, :] * u + w_ref[1:2, :] * _shift_rows(u, tail, 1)
             + w_ref[0:1, :] * _shift_rows(u, tail, 2) + b_ref[...])
        t_ref[...] = u[-SUBLANES:]
        return y

    g = conv(g_ref, gt_ref, wg_ref, bg_ref)
    v = conv(v_ref, vt_ref, wv_ref, bv_ref)
    o_ref[...] = (g * jax.nn.sigmoid(g) * v).astype(o_ref.dtype)


def ffn_conv_act(u, conv_w, conv_b, bsz, out_dtype=BF16, ts=512, tc=512):
    m, f2 = u.shape
    f = f2 // 2
    s = m // bsz
    ts = _pick_tile(s, ts, SUBLANES)
    tc = _pick_tile(f, tc, LANES)
    ncb = f // tc
    nsb = s // ts
    row_map = lambda c, b, t: (b * nsb + t, c)
    row_map_v = lambda c, b, t: (b * nsb + t, c + ncb)
    w_map = lambda c, b, t: (0, c)
    w_map_v = lambda c, b, t: (0, c + ncb)
    cb = conv_b.reshape(1, f2).astype(F32)
    cw = conv_w.astype(F32)
    return pl.pallas_call(
        _ffn_act_kernel,
        grid=(ncb, bsz, nsb),
        in_specs=[pl.BlockSpec((ts, tc), row_map), pl.BlockSpec((ts, tc), row_map_v),
                  pl.BlockSpec((conv_w.shape[0], tc), w_map), pl.BlockSpec((conv_w.shape[0], tc), w_map_v),
                  pl.BlockSpec((1, tc), w_map), pl.BlockSpec((1, tc), w_map_v)],
        out_specs=pl.BlockSpec((ts, tc), row_map),
        out_shape=jax.ShapeDtypeStruct((m, f), out_dtype),
        scratch_shapes=[pltpu.VMEM((SUBLANES, tc), F32), pltpu.VMEM((SUBLANES, tc), F32)],
        compiler_params=_cparams("parallel", "parallel", "arbitrary"),
    )(u, u, cw, cw, cb, cb)


def _rms_norm_jnp(x, g):
    xf = x.astype(F32)
    y = xf * lax.rsqrt(jnp.mean(xf * xf, axis=-1, keepdims=True) + NORM_EPS)
    return (y * g.astype(F32)).astype(x.dtype)


def _to_chunks(t, chunk, heads, dim):
    b, s, _ = t.shape
    return t.reshape(b, s // chunk, chunk, heads, dim).transpose(1, 0, 3, 2, 4).astype(F32)


def _from_chunks(t):
    nc, b, h, l, d = t.shape
    return t.transpose(1, 0, 3, 2, 4).reshape(b, nc * l, h, d)


def _causal_dwconv(x, w, b):
    width, ch = w.shape
    y = lax.conv_general_dilated(x, w[:, None, :].astype(x.dtype), window_strides=(1,),
                                 padding=[(width - 1, 0)],
                                 dimension_numbers=('NWC', 'WIO', 'NWC'),
                                 feature_group_count=ch)
    return y + b


def mlstm_mixer(hn, bsz, w_in, b_gates, w_norm, w_out, x_res):
    m, d = hn.shape
    s = m // bsz
    H = b_gates.shape[0] // 2
    DV = w_norm.shape[0] // H
    DK = DV // 2
    L = MLSTM_CHUNK
    nc = s // L
    n_main = 2 * H * DK + 2 * H * DV
    w_main = w_in[:, :n_main].astype(BF16)
    w_g = jnp.pad(w_in[:, n_main:], ((0, 0), (0, LANES - 2 * H))).astype(BF16)
    proj = matmul(hn, w_main).reshape(bsz, s, n_main)
    gates = matmul(hn, w_g, tn=LANES)[:, :2 * H].reshape(bsz, s, 2 * H)
    q, k, v, o = jnp.split(proj, [H * DK, 2 * H * DK, 2 * H * DK + H * DV], axis=-1)
    gates = (gates + b_gates).astype(F32)
    ic = gates[..., :H].reshape(bsz, nc, L, H).transpose(1, 0, 3, 2)
    fc = jax.nn.log_sigmoid(gates[..., H:]).reshape(bsz, nc, L, H).transpose(1, 0, 3, 2)
    qc = _to_chunks(q, L, H, DK) * (DK ** -0.5)
    kc = _to_chunks(k, L, H, DK)
    vc = _to_chunks(v, L, H, DV)
    causal = jnp.tril(jnp.ones((L, L), dtype=bool))

    def step(carry, inp):
        C, n, mm = carry
        q_, k_, v_, i_, lf = inp
        b = jnp.cumsum(lf, axis=-1)
        a = b + mm[..., None]
        dd = jnp.where(causal, b[..., :, None] - b[..., None, :] + i_[..., None, :], -jnp.inf)
        m_t = jnp.maximum(a, jnp.max(dd, axis=-1))
        w_intra = jnp.exp(dd - m_t[..., None])
        w_inter = jnp.exp(a - m_t)
        sc = jnp.einsum('bhtk,bhsk->bhts', q_, k_) * w_intra
        num = jnp.einsum('bhts,bhsv->bhtv', sc, v_) + w_inter[..., None] * jnp.einsum('bhtk,bhkv->bhtv', q_, C)
        den = jnp.sum(sc, axis=-1) + w_inter * jnp.einsum('bhtk,bhk->bht', q_, n)
        out = num / jnp.maximum(jnp.abs(den), jnp.exp(-m_t))[..., None]
        b_last = b[..., -1]
        g = b_last[..., None] - b + i_
        m_new = jnp.maximum(b_last + mm, jnp.max(g, axis=-1))
        ws = jnp.exp(g - m_new[..., None])
        decay = jnp.exp(b_last + mm - m_new)
        C_new = decay[..., None, None] * C + jnp.einsum('bhsk,bhsv->bhkv', ws[..., None] * k_, v_)
        n_new = decay[..., None] * n + jnp.einsum('bhs,bhsk->bhk', ws, k_)
        return (C_new, n_new, m_new), out

    init = (jnp.zeros((bsz, H, DK, DV), F32), jnp.zeros((bsz, H, DK), F32), jnp.zeros((bsz, H), F32))
    _, hc = lax.scan(step, init, (qc, kc, vc, ic, fc))
    hh = _rms_norm_jnp(_from_chunks(hc), w_norm.reshape(H, DV))
    hh = jax.nn.sigmoid(o.astype(F32)) * hh.reshape(bsz, s, H * DV)
    return matmul(hh.reshape(m, H * DV).astype(BF16), w_out.astype(BF16), residual=x_res)


def diff_attn_mixer(hn, bsz, w_in, lam_q1, lam_k1, lam_q2, lam_k2, w_subln, w_out, lambda_init, x_res):
    m, d = hn.shape
    s = m // bsz
    DH = DIFF_DH
    H = d // (2 * DH)
    nb = s // Q_BLOCK
    proj = matmul(hn, w_in.astype(BF16)).reshape(bsz, s, 3 * d)
    q, k, v = jnp.split(proj, 3, axis=-1)
    q = q.reshape(bsz, s, H, 2, DH).transpose(0, 2, 3, 1, 4)
    k = k.reshape(bsz, s, H, 2, DH).transpose(0, 2, 3, 1, 4)
    v = v.reshape(bsz, s, H, 2 * DH).transpose(0, 2, 1, 3)
    lam = (jnp.exp(jnp.sum((lam_q1 * lam_k1).astype(F32)))
           - jnp.exp(jnp.sum((lam_q2 * lam_k2).astype(F32))) + lambda_init)
    slopes = jnp.exp2(-8.0 * jnp.arange(1, H + 1, dtype=F32) / H)
    qb = q.reshape(bsz, H, 2, nb, Q_BLOCK, DH).transpose(3, 0, 1, 2, 4, 5)
    pos_k = jnp.arange(s)
    scale = DH ** -0.5

    def block(args):
        qi, idx = args
        pos_q = idx * Q_BLOCK + jnp.arange(Q_BLOCK)
        dist = (pos_q[:, None] - pos_k[None, :]).astype(F32)
        bias = jnp.where(dist >= 0, -slopes[:, None, None] * dist, -jnp.inf)
        sc = jnp.einsum('bhcqd,bhcsd->bhcqs', qi, k).astype(F32) * scale + bias[None, :, None]
        p = jax.nn.softmax(sc, axis=-1)
        attn = p[:, :, 0] - lam * p[:, :, 1]
        return jnp.einsum('bhqs,bhsv->bhqv', attn.astype(v.dtype), v)

    o = lax.map(block, (qb, jnp.arange(nb)))
    o = o.transpose(1, 0, 3, 2, 4).reshape(bsz, s, H, 2 * DH)
    o = _rms_norm_jnp(o, w_subln).astype(F32) * (1.0 - lambda_init)
    return matmul(o.reshape(m, d).astype(BF16), w_out.astype(BF16), residual=x_res)


def mamba2_mixer(hn, bsz, w_in, conv_w, conv_b, dt_bias, a_log, d_skip, w_norm, w_out, x_res):
    m, d = hn.shape
    s = m // bsz
    NH = dt_bias.shape[0]
    P, N, G, L = SSM_HEADDIM, SSM_STATE, SSM_GROUPS, SSM_CHUNK
    DI = NH * P
    HPG = NH // G
    GN = G * N
    nc = s // L
    n_main = 2 * DI + 2 * GN
    proj = matmul(hn, w_in[:, :n_main].astype(BF16)).reshape(bsz, s, n_main)
    dt = matmul(hn, w_in[:, n_main:].astype(BF16), tn=LANES).reshape(bsz, s, NH)
    z, xbc = jnp.split(proj, [DI], axis=-1)
    xbc = jax.nn.silu(_causal_dwconv(xbc, conv_w, conv_b))
    xs, Bm, Cm = jnp.split(xbc, [DI, DI + GN], axis=-1)
    dt = jax.nn.softplus((dt + dt_bias).astype(F32))
    A = -jnp.exp(a_log.astype(F32)).reshape(G, HPG)
    xc = xs.reshape(bsz, nc, L, G, HPG, P).transpose(1, 0, 2, 3, 4, 5).astype(F32)
    Bc = Bm.reshape(bsz, nc, L, G, N).transpose(1, 0, 2, 3, 4).astype(F32)
    Cc = Cm.reshape(bsz, nc, L, G, N).transpose(1, 0, 2, 3, 4).astype(F32)
    dtc = dt.reshape(bsz, nc, L, G, HPG).transpose(1, 0, 2, 3, 4)
    causal = jnp.tril(jnp.ones((L, L), dtype=bool))[None, :, :, None, None]

    def step(state, inp):
        x_, B_, C_, dt_ = inp
        cum = jnp.cumsum(dt_ * A, axis=1)
        seg = cum[:, :, None] - cum[:, None, :]
        decay = jnp.exp(jnp.where(causal, seg, -jnp.inf))
        cb = jnp.einsum('btgn,bsgn->btsg', C_, B_)
        w = cb[..., None] * decay * dt_[:, None]
        y_intra = jnp.einsum('btsgh,bsghp->btghp', w, x_)
        y_inter = jnp.einsum('btgn,bghpn->btghp', C_, state) * jnp.exp(cum)[..., None]
        w_last = jnp.exp(cum[:, -1:] - cum) * dt_
        state_new = (jnp.exp(cum[:, -1])[..., None, None] * state
                     + jnp.einsum('bsgn,bsghp->bghpn', B_, w_last[..., None] * x_))
        return state_new, y_intra + y_inter

    init = jnp.zeros((bsz, G, HPG, P, N), F32)
    _, yc = lax.scan(step, init, (xc, Bc, Cc, dtc))
    y = yc.transpose(1, 0, 2, 3, 4, 5).reshape(bsz, s, NH, P)
    y = y + xs.reshape(bsz, s, NH, P).astype(F32) * d_skip.astype(F32)[:, None]
    y = y.reshape(bsz, s, DI) * jax.nn.silu(z.astype(F32))
    y = _rms_norm_jnp(y.reshape(bsz, s, G, DI // G), w_norm.reshape(G, DI // G))
    return matmul(y.reshape(m, DI).astype(BF16), w_out.astype(BF16), residual=x_res)


def hgrn2_mixer(hn, bsz, w_in, lower_bound, w_gnorm, w_out, x_res):
    m, d = hn.shape
    s = m // bsz
    DK, L = HGRN_DK, HGRN_CHUNK
    H = d // DK
    DV = d // H
    proj = matmul(hn, w_in.astype(BF16)).reshape(bsz, s, 4 * d)
    q, f, i, g = jnp.split(proj, 4, axis=-1)
    q = jax.nn.silu(q)
    f = f.astype(F32)
    lb = lower_bound.astype(F32)
    key = (1.0 - lb) * jax.nn.sigmoid(-f)
    log_f = jnp.logaddexp(jnp.log(lb), jnp.log1p(-lb) + jax.nn.log_sigmoid(f))
    qc = _to_chunks(q, L, H, DK)
    kc = _to_chunks(key, L, H, DK)
    vc = _to_chunks(i, L, H, DV)
    fc = _to_chunks(log_f, L, H, DK)
    causal = jnp.tril(jnp.ones((L, L), dtype=bool))[:, :, None]

    def step(S, inp):
        q_, k_, v_, lf = inp
        b = jnp.cumsum(lf, axis=2)
        rel = b[:, :, :, None] - b[:, :, None, :]
        dec = jnp.exp(jnp.where(causal, rel, -jnp.inf))
        attn = jnp.sum(q_[:, :, :, None] * dec * k_[:, :, None], axis=-1)
        o = (jnp.einsum('bhts,bhsv->bhtv', attn, v_)
             + jnp.einsum('bhtk,bhkv->bhtv', q_ * jnp.exp(b), S))
        b_last = b[:, :, -1]
        S_new = (jnp.exp(b_last)[..., None] * S
                 + jnp.einsum('bhsk,bhsv->bhkv', k_ * jnp.exp(b_last[:, :, None] - b), v_))
        return S_new, o

    _, oc = lax.scan(step, jnp.zeros((bsz, H, DK, DV), F32), (qc, kc, vc, fc))
    o = _rms_norm_jnp(_from_chunks(oc), w_gnorm).astype(F32)
    o = o * jax.nn.silu(g.astype(F32)).reshape(bsz, s, H, DV)
    return matmul(o.reshape(m, d).astype(BF16), w_out.astype(BF16), residual=x_res)


def conv_ffn(x2d, bsz, g_norm, w_up, conv_w, conv_b, w_down):
    hn = rmsnorm(x2d, g_norm)
    u = matmul(hn, w_up.astype(BF16))
    a = ffn_conv_act(u, conv_w, conv_b, bsz)
    return matmul(a, w_down.astype(BF16), residual=x2d)


def kernel(x, norm_mix, norm_ffn, norm_final, mlstm_w_in, mlstm_b_gates, mlstm_w_norm, mlstm_w_out, diff_w_in, diff_lam_q1, diff_lam_k1, diff_lam_q2, diff_lam_k2, diff_w_subln, diff_w_out, ssm_w_in, ssm_conv_w, ssm_conv_b, ssm_dt_bias, ssm_a_log, ssm_d, ssm_w_norm, ssm_w_out, hgrn_w_in, hgrn_lb, hgrn_w_gnorm, hgrn_w_out, ffn_w_up, ffn_conv_w, ffn_conv_b, ffn_w_down):
    bsz, s, d = x.shape
    depth = norm_mix.shape[0]
    lb_soft = jax.nn.softmax(hgrn_lb.astype(F32), axis=0)
    lower_bounds = jnp.cumsum(lb_soft, axis=0) - lb_soft[0]
    x2d = x.reshape(bsz * s, d)
    for layer in range(depth):
        kind, j = layer % N_MIXERS, layer // N_MIXERS
        hn = rmsnorm(x2d, norm_mix[layer])
        if kind == 0:
            x2d = mlstm_mixer(hn, bsz, mlstm_w_in[j], mlstm_b_gates[j], mlstm_w_norm[j], mlstm_w_out[j], x2d)
        elif kind == 1:
            lambda_init = 0.8 - 0.6 * math.exp(-0.3 * layer)
            x2d = diff_attn_mixer(hn, bsz, diff_w_in[j], diff_lam_q1[j], diff_lam_k1[j], diff_lam_q2[j],
                                  diff_lam_k2[j], diff_w_subln[j], diff_w_out[j], lambda_init, x2d)
        elif kind == 2:
            x2d = mamba2_mixer(hn, bsz, ssm_w_in[j], ssm_conv_w[j], ssm_conv_b[j], ssm_dt_bias[j],
                               ssm_a_log[j], ssm_d[j], ssm_w_norm[j], ssm_w_out[j], x2d)
        else:
            x2d = hgrn2_mixer(hn, bsz, hgrn_w_in[j], lower_bounds[layer], hgrn_w_gnorm[j], hgrn_w_out[j], x2d)
        x2d = conv_ffn(x2d, bsz, norm_ffn[layer], ffn_w_up[layer], ffn_conv_w[layer], ffn_conv_b[layer],
                       ffn_w_down[layer])
    return rmsnorm(x2d, norm_final, out_dtype=x.dtype).reshape(bsz, s, d)
```

```python
import functools
import math

import jax
import jax.numpy as jnp
from jax import lax
from jax.experimental import pallas as pl
from jax.experimental.pallas import tpu as pltpu

NORM_EPS = 1e-6
N_MIXERS = 4
DIFF_DH = 128
SSM_HEADDIM = 64
SSM_STATE = 128
SSM_GROUPS = 8
SSM_CHUNK = 64
HGRN_DK = 128
HGRN_CHUNK = 64
LOG2E = 1.4426950408889634
NEG_BIG = -1e30

V7X_VMEM_BYTES = 64 * 1024 * 1024
VMEM_LIMIT_BYTES = 56 * 1024 * 1024
LANES = 128
SUBLANES = 8

F32 = jnp.float32
BF16 = jnp.bfloat16


def _cparams(*sem):
    return pltpu.CompilerParams(dimension_semantics=sem, vmem_limit_bytes=VMEM_LIMIT_BYTES)


def _mm_kernel(x_ref, w_ref, *rest, has_res, nk):
    if has_res:
        r_ref, o_ref = rest
    else:
        (o_ref,) = rest
    part = jnp.dot(x_ref[...], w_ref[...], preferred_element_type=F32)
    if nk == 1:
        if has_res:
            part = part + r_ref[...]
        o_ref[...] = part.astype(o_ref.dtype)
    else:
        k = pl.program_id(2)

        @pl.when(k == 0)
        def _():
            if has_res:
                o_ref[...] = r_ref[...] + part
            else:
                o_ref[...] = part

        @pl.when(k > 0)
        def _():
            o_ref[...] += part


def _pick_tile(n, target, quantum):
    best = None
    t = quantum
    while t <= min(n, target):
        if n % t == 0:
            best = t
        t += quantum
    assert best is not None, (n, target, quantum)
    return best


def matmul(x, w, residual=None, out_dtype=F32, tm=1024, tn=512, tk=4096):
    m, k = x.shape
    k2, n = w.shape
    assert k == k2
    tm = _pick_tile(m, tm, SUBLANES)
    tn = _pick_tile(n, tn, LANES)
    tk = k if k <= tk else _pick_tile(k, max(tk, 5504), LANES)
    nk = k // tk
    if nk > 1:
        assert out_dtype == F32
    has_res = residual is not None
    in_specs = [pl.BlockSpec((tm, tk), lambda i, j, kk: (i, kk)),
                pl.BlockSpec((tk, tn), lambda i, j, kk: (kk, j))]
    args = [x, w]
    if has_res:
        in_specs.append(pl.BlockSpec((tm, tn), lambda i, j, kk: (i, j)))
        args.append(residual)
    return pl.pallas_call(
        functools.partial(_mm_kernel, has_res=has_res, nk=nk),
        grid=(m // tm, n // tn, nk),
        in_specs=in_specs,
        out_specs=pl.BlockSpec((tm, tn), lambda i, j, kk: (i, j)),
        out_shape=jax.ShapeDtypeStruct((m, n), out_dtype),
        compiler_params=_cparams("parallel", "parallel", "arbitrary"),
        name="matmul",
    )(*args)


def _rmsnorm_kernel(x_ref, g_ref, o_ref):
    x = x_ref[...].astype(F32)
    ms = jnp.mean(x * x, axis=-1, keepdims=True)
    o_ref[...] = (x * lax.rsqrt(ms + NORM_EPS) * g_ref[...]).astype(o_ref.dtype)


def rmsnorm(x, g, out_dtype=BF16, tm=512):
    m, d = x.shape
    tm = _pick_tile(m, tm, SUBLANES)
    return pl.pallas_call(
        _rmsnorm_kernel,
        grid=(m // tm,),
        in_specs=[pl.BlockSpec((tm, d), lambda i: (i, 0)),
                  pl.BlockSpec((1, d), lambda i: (0, 0))],
        out_specs=pl.BlockSpec((tm, d), lambda i: (i, 0)),
        out_shape=jax.ShapeDtypeStruct((m, d), out_dtype),
        compiler_params=_cparams("parallel"),
        name="rmsnorm",
    )(x, g.reshape(1, d).astype(F32))


def _shift_rows(cur, prev_tail, shift):
    rolled = pltpu.roll(cur, shift, axis=0)
    tail = pltpu.roll(prev_tail, shift, axis=0)
    row = lax.broadcasted_iota(jnp.int32, (SUBLANES, cur.shape[1]), 0)
    head = jnp.where(row < shift, tail, rolled[:SUBLANES])
    return jnp.concatenate([head, rolled[SUBLANES:]], axis=0)


def _ffn_act_kernel(g_ref, v_ref, wg_ref, wv_ref, bg_ref, bv_ref, o_ref, gt_ref, vt_ref):
    s = pl.program_id(2)

    @pl.when(s == 0)
    def _():
        gt_ref[...] = jnp.zeros_like(gt_ref)
        vt_ref[...] = jnp.zeros_like(vt_ref)

    def conv(u_ref, t_ref, w_ref, b_ref):
        u = u_ref[...].astype(F32)
        tail = t_ref[...]
        y = (w_ref[2:3, :] * u + w_ref[1:2, :] * _shift_rows(u, tail, 1)
             + w_ref[0:1, :] * _shift_rows(u, tail, 2) + b_ref[...])
        t_ref[...] = u[-SUBLANES:]
        return y

    g = conv(g_ref, gt_ref, wg_ref, bg_ref)
    v = conv(v_ref, vt_ref, wv_ref, bv_ref)
    o_ref[...] = (g * jax.nn.sigmoid(g) * v).astype(o_ref.dtype)


def ffn_conv_act(u, conv_w, conv_b, bsz, out_dtype=BF16, ts=512, tc=512):
    m, f2 = u.shape
    f = f2 // 2
    s = m // bsz
    ts = _pick_tile(s, ts, SUBLANES)
    tc = _pick_tile(f, tc, LANES)
    ncb = f // tc
    nsb = s // ts
    row_map = lambda c, b, t: (b * nsb + t, c)
    row_map_v = lambda c, b, t: (b * nsb + t, c + ncb)
    w_map = lambda c, b, t: (0, c)
    w_map_v = lambda c, b, t: (0, c + ncb)
    cb = conv_b.reshape(1, f2).astype(F32)
    cw = conv_w.astype(F32)
    return pl.pallas_call(
        _ffn_act_kernel,
        grid=(ncb, bsz, nsb),
        in_specs=[pl.BlockSpec((ts, tc), row_map), pl.BlockSpec((ts, tc), row_map_v),
                  pl.BlockSpec((conv_w.shape[0], tc), w_map), pl.BlockSpec((conv_w.shape[0], tc), w_map_v),
                  pl.BlockSpec((1, tc), w_map), pl.BlockSpec((1, tc), w_map_v)],
        out_specs=pl.BlockSpec((ts, tc), row_map),
        out_shape=jax.ShapeDtypeStruct((m, f), out_dtype),
        scratch_shapes=[pltpu.VMEM((SUBLANES, tc), F32), pltpu.VMEM((SUBLANES, tc), F32)],
        compiler_params=_cparams("parallel", "parallel", "arbitrary"),
        name="ffn_conv_act",
    )(u, u, cw, cw, cb, cb)


def _lane_cumsum(x):
    lane = lax.broadcasted_iota(jnp.int32, x.shape, 1)
    shift = 1
    while shift < x.shape[1]:
        x = x + jnp.where(lane >= shift, pltpu.roll(x, shift, axis=1), 0.0)
        shift *= 2
    return x


def _mlstm_kernel(q_ref, k_ref, v_ref, og_ref, gi_ref, gf_ref, bi_ref, bf_ref, wn_ref, out_ref,
                  c_ref, n_ref, m_ref, *, n_chunks):
    L = LANES
    dk = q_ref.shape[1]
    dv = v_ref.shape[1]

    @pl.when(pl.program_id(2) == 0)
    def _():
        c_ref[...] = jnp.zeros_like(c_ref)
        n_ref[...] = jnp.zeros_like(n_ref)
        m_ref[...] = jnp.zeros_like(m_ref)

    row_i = lax.broadcasted_iota(jnp.int32, (L, L), 0)
    col_i = lax.broadcasted_iota(jnp.int32, (L, L), 1)
    causal = col_i <= row_i
    bias_i = bi_ref[0]
    bias_f = bf_ref[0]

    for c in range(n_chunks):
        rows = slice(c * L, (c + 1) * L)
        q = (q_ref[rows, :].astype(F32) * (dk ** -0.5)).astype(BF16)
        k = k_ref[rows, :]
        v = v_ref[rows, :]
        i_row = gi_ref[0, :, rows] + bias_i
        f_raw = gf_ref[0, :, rows] + bias_f
        lf_row = jnp.minimum(f_raw, 0.0) - jnp.log1p(jnp.exp(-jnp.abs(f_raw)))
        b_row = _lane_cumsum(lf_row)
        b_last = jnp.min(b_row, axis=1, keepdims=True)
        bm = jnp.broadcast_to(b_row, (L, L))
        im = jnp.broadcast_to(i_row, (L, L))
        bt = bm.T
        it = im.T
        m_prev = m_ref[0:1, :]
        d = jnp.where(causal, bt - bm + im, NEG_BIG)
        a = bt + m_prev
        m_t = jnp.maximum(a, jnp.max(d, axis=1, keepdims=True))
        w_intra = jnp.exp(d - m_t)
        w_inter = jnp.exp(a - m_t)
        sc = lax.dot_general(q, k, (((1,), (1,)), ((), ())), preferred_element_type=F32) * w_intra
        num = (jnp.dot(sc.astype(BF16), v, preferred_element_type=F32)
               + pltpu.repeat(w_inter, dv // LANES, axis=1)
               * jnp.dot(q, c_ref[...].astype(BF16), preferred_element_type=F32))
        qn = jnp.sum(q.astype(F32) * n_ref[0:1, :], axis=1, keepdims=True)
        den = jnp.sum(sc, axis=1, keepdims=True) + w_inter * qn
        den = jnp.maximum(jnp.abs(den), jnp.exp(-m_t))
        hc = num / pltpu.repeat(den, dv // LANES, axis=1)
        hc = hc * lax.rsqrt(jnp.mean(hc * hc, axis=1, keepdims=True) + NORM_EPS) * wn_ref[...]
        out_ref[rows, :] = (jax.nn.sigmoid(og_ref[rows, :].astype(F32)) * hc).astype(out_ref.dtype)

        g = b_last - bt + it
        m_new = jnp.maximum(b_last + m_prev, jnp.max(g, axis=0, keepdims=True))
        ws = jnp.exp(g - m_new)
        decay = jnp.exp(b_last + m_prev - m_new)
        kw = k.astype(F32) * pltpu.repeat(ws, dk // LANES, axis=1)
        c_ref[...] = (pltpu.repeat(decay, dv // LANES, axis=1) * c_ref[...]
                      + jnp.dot(kw.T.astype(BF16), v, preferred_element_type=F32))
        n_ref[...] = (pltpu.repeat(decay, dk // LANES, axis=1) * n_ref[...]
                      + jnp.sum(kw, axis=0, keepdims=True))
        m_ref[...] = jnp.broadcast_to(m_new, m_ref.shape)


def mlstm_core(proj, gates_t, bsz, b_gates, w_norm, t_rows=512):
    m = proj.shape[0]
    s = m // bsz
    n_heads = b_gates.shape[0] // 2
    dv = w_norm.shape[0] // n_heads
    dk = dv // 2
    t_rows = min(t_rows, s)
    assert s % t_rows == 0 and t_rows % LANES == 0 and dk % LANES == 0
    nt = s // t_rows
    bg = jnp.broadcast_to(b_gates.astype(F32)[:, None, None], (2 * n_heads, 1, LANES))
    row = lambda b, h, t: b * nt + t
    return pl.pallas_call(
        functools.partial(_mlstm_kernel, n_chunks=t_rows // LANES),
        grid=(bsz, n_heads, nt),
        in_specs=[pl.BlockSpec((t_rows, dk), lambda b, h, t: (row(b, h, t), h)),
                  pl.BlockSpec((t_rows, dk), lambda b, h, t: (row(b, h, t), n_heads + h)),
                  pl.BlockSpec((t_rows, dv), lambda b, h, t: (row(b, h, t), n_heads + h)),
                  pl.BlockSpec((t_rows, dv), lambda b, h, t: (row(b, h, t), 2 * n_heads + h)),
                  pl.BlockSpec((1, 1, t_rows), lambda b, h, t: (h, 0, row(b, h, t))),
                  pl.BlockSpec((1, 1, t_rows), lambda b, h, t: (n_heads + h, 0, row(b, h, t))),
                  pl.BlockSpec((1, 1, LANES), lambda b, h, t: (h, 0, 0)),
                  pl.BlockSpec((1, 1, LANES), lambda b, h, t: (n_heads + h, 0, 0)),
                  pl.BlockSpec((1, dv), lambda b, h, t: (0, h))],
        out_specs=pl.BlockSpec((t_rows, dv), lambda b, h, t: (row(b, h, t), h)),
        out_shape=jax.ShapeDtypeStruct((m, n_heads * dv), BF16),
        scratch_shapes=[pltpu.VMEM((dk, dv), F32), pltpu.VMEM((SUBLANES, dk), F32),
                        pltpu.VMEM((SUBLANES, LANES), F32)],
        compiler_params=_cparams("parallel", "parallel", "arbitrary"),
        name="mlstm_core",
    )(proj, proj, proj, proj, gates_t, gates_t, bg, bg, w_norm.reshape(1, n_heads * dv).astype(F32))


def mlstm_mixer(hn, bsz, w_in, b_gates, w_norm, w_out, x_res):
    n_heads = b_gates.shape[0] // 2
    n_main = w_in.shape[1] - 2 * n_heads
    w_g = jnp.pad(w_in[:, n_main:], ((0, 0), (0, LANES - 2 * n_heads))).astype(BF16)
    proj = matmul(hn, w_in[:, :n_main].astype(BF16), out_dtype=BF16)
    gates_t = matmul(hn, w_g, tn=LANES)[:, :2 * n_heads].T[:, None, :]
    hh = mlstm_core(proj, gates_t, bsz, b_gates, w_norm)
    return matmul(hh, w_out.astype(BF16), residual=x_res)


def _diff_attn_kernel(q_ref, k_ref, v_ref, sl_ref, lq1_ref, lk1_ref, lq2_ref, lk2_ref, g_ref, o_ref,
                      m_ref, l_ref, acc_ref, *, tq, tk, lambda_init):
    qi = pl.program_id(2)
    dh = DIFF_DH
    q = (q_ref[...].astype(F32) * (dh ** -0.5 * LOG2E)).astype(BF16)
    slope_row = pltpu.repeat(sl_ref[0], tk // LANES, axis=1)
    col = lax.broadcasted_iota(jnp.int32, (1, tk), 1).astype(F32)
    m_ref[...] = jnp.full(m_ref.shape, NEG_BIG, F32)
    l_ref[...] = jnp.zeros(l_ref.shape, F32)
    acc_ref[...] = jnp.zeros(acc_ref.shape, F32)

    def step(kk, masked):
        start = pl.multiple_of(kk * tk, tk)
        cb = (col + (kk * tk - qi * tq).astype(F32)) * slope_row
        kblk = k_ref[pl.ds(start, tk), :]
        vblk = v_ref[pl.ds(start, tk), :]
        for c in range(2):
            s = lax.dot_general(q[:, c * dh:(c + 1) * dh], kblk[:, c * dh:(c + 1) * dh],
                                (((1,), (1,)), ((), ())), preferred_element_type=F32)
            s = s + cb
            if masked:
                row_i = lax.broadcasted_iota(jnp.int32, (tq, tk), 0)
                col_i = lax.broadcasted_iota(jnp.int32, (tq, tk), 1)
                s = jnp.where(col_i <= row_i, s, NEG_BIG)
            m_prev = m_ref[c]
            m_new = jnp.maximum(m_prev, jnp.max(s, axis=1, keepdims=True))
            alpha = jnp.exp2(m_prev - m_new)
            p = jnp.exp2(s - pltpu.repeat(m_new, tk // LANES, axis=1))
            l_ref[c] = alpha * l_ref[c] + jnp.sum(p, axis=1, keepdims=True)
            m_ref[c] = m_new
            acc_ref[c] = (acc_ref[c] * pltpu.repeat(alpha, 2 * dh // LANES, axis=1)
                          + jnp.dot(p.astype(BF16), vblk, preferred_element_type=F32))

    def body(kk, carry):
        step(kk, False)
        return carry

    lax.fori_loop(0, qi, body, 0)
    step(qi, True)

    lam = (jnp.exp(jnp.sum(lq1_ref[...] * lk1_ref[...])) - jnp.exp(jnp.sum(lq2_ref[...] * lk2_ref[...]))
           + lambda_init)
    rep = 2 * dh // LANES
    o = (acc_ref[0] / pltpu.repeat(l_ref[0], rep, axis=1)
         - lam * (acc_ref[1] / pltpu.repeat(l_ref[1], rep, axis=1)))
    o = o * lax.rsqrt(jnp.mean(o * o, axis=-1, keepdims=True) + NORM_EPS) * g_ref[...]
    o_ref[...] = (o * (1.0 - lambda_init)).astype(o_ref.dtype)


def diff_attn_core(proj, bsz, lam_q1, lam_k1, lam_q2, lam_k2, w_subln, lambda_init, tq=512):
    m, d3 = proj.shape
    d = d3 // 3
    s = m // bsz
    hw = 2 * DIFF_DH
    n_heads = d // hw
    tq = min(tq, s)
    assert s % tq == 0
    nq = s // tq
    slopes = jnp.exp2(-8.0 * jnp.arange(1, n_heads + 1, dtype=F32) / n_heads) * LOG2E
    slopes = jnp.broadcast_to(slopes[:, None, None], (n_heads, 1, LANES))
    vec = lambda a: a.reshape(1, -1).astype(F32)
    small = pl.BlockSpec((1, DIFF_DH), lambda b, h, i: (0, 0))
    return pl.pallas_call(
        functools.partial(_diff_attn_kernel, tq=tq, tk=tq, lambda_init=lambda_init),
        grid=(bsz, n_heads, nq),
        in_specs=[pl.BlockSpec((tq, hw), lambda b, h, i: (b * nq + i, h)),
                  pl.BlockSpec((s, hw), lambda b, h, i: (b, n_heads + h)),
                  pl.BlockSpec((s, hw), lambda b, h, i: (b, 2 * n_heads + h)),
                  pl.BlockSpec((1, 1, LANES), lambda b, h, i: (h, 0, 0)),
                  small, small, small, small,
                  pl.BlockSpec((1, hw), lambda b, h, i: (0, 0))],
        out_specs=pl.BlockSpec((tq, hw), lambda b, h, i: (b * nq + i, h)),
        out_shape=jax.ShapeDtypeStruct((m, d), BF16),
        scratch_shapes=[pltpu.VMEM((2, tq, LANES), F32), pltpu.VMEM((2, tq, LANES), F32),
                        pltpu.VMEM((2, tq, hw), F32)],
        compiler_params=_cparams("parallel", "parallel", "arbitrary"),
        name="diff_attn",
    )(proj, proj, proj, slopes, vec(lam_q1), vec(lam_k1), vec(lam_q2), vec(lam_k2), vec(w_subln))


def diff_attn_mixer(hn, bsz, w_in, lam_q1, lam_k1, lam_q2, lam_k2, w_subln, w_out, lambda_init, x_res):
    proj = matmul(hn, w_in.astype(BF16), out_dtype=BF16)
    o = diff_attn_core(proj, bsz, lam_q1, lam_k1, lam_q2, lam_k2, w_subln, lambda_init)
    return matmul(o, w_out.astype(BF16), residual=x_res)


def _dwconv_silu_kernel(u_ref, w_ref, b_ref, o_ref, tail_ref, *, width):
    @pl.when(pl.program_id(2) == 0)
    def _():
        tail_ref[...] = jnp.zeros_like(tail_ref)

    u = u_ref[...].astype(F32)
    tail = tail_ref[...]
    y = w_ref[width - 1:width, :] * u + b_ref[...]
    for j in range(width - 1):
        y = y + w_ref[j:j + 1, :] * _shift_rows(u, tail, width - 1 - j)
    tail_ref[...] = u[-SUBLANES:]
    o_ref[...] = (y * jax.nn.sigmoid(y)).astype(o_ref.dtype)


def dwconv_silu(u, col0, conv_w, conv_b, bsz, ts=512, tc=512):
    m = u.shape[0]
    width, ch = conv_w.shape
    s = m // bsz
    ts = _pick_tile(s, ts, SUBLANES)
    tc = _pick_tile(math.gcd(ch, col0) if col0 else ch, tc, LANES)
    nsb = s // ts
    c0 = col0 // tc
    return pl.pallas_call(
        functools.partial(_dwconv_silu_kernel, width=width),
        grid=(ch // tc, bsz, nsb),
        in_specs=[pl.BlockSpec((ts, tc), lambda c, b, t: (b * nsb + t, c0 + c)),
                  pl.BlockSpec((width, tc), lambda c, b, t: (0, c)),
                  pl.BlockSpec((1, tc), lambda c, b, t: (0, c))],
        out_specs=pl.BlockSpec((ts, tc), lambda c, b, t: (b * nsb + t, c)),
        out_shape=jax.ShapeDtypeStruct((m, ch), BF16),
        scratch_shapes=[pltpu.VMEM((SUBLANES, tc), F32)],
        compiler_params=_cparams("parallel", "parallel", "arbitrary"),
        name="dwconv_silu",
    )(u, conv_w.astype(F32), conv_b.reshape(1, ch).astype(F32))


def _split3(v):
    hi = v.astype(BF16)
    r = v - hi.astype(F32)
    mid = r.astype(BF16)
    lo = (r - mid.astype(F32)).astype(BF16)
    return hi, mid, lo


def _dot_exact_rhs(lhs_f32, rhs_bf16):
    n = lhs_f32.shape[0]
    out = jnp.dot(jnp.concatenate(_split3(lhs_f32), axis=0), rhs_bf16, preferred_element_type=F32)
    return out[:n] + out[n:2 * n] + out[2 * n:]


def _dot_exact_lhs(lhs_bf16, rhs_f32):
    hi, mid, lo = _split3(rhs_f32)
    return (jnp.dot(lhs_bf16, hi, preferred_element_type=F32) + jnp.dot(lhs_bf16, mid, preferred_element_type=F32)
            + jnp.dot(lhs_bf16, lo, preferred_element_type=F32))


def _softplus(x):
    return jnp.maximum(x, 0.0) + jnp.log1p(jnp.exp(-jnp.abs(x)))


def _ssd_kernel(x_ref, b_ref, c_ref, z_ref, dtc_ref, dtp_ref, dtbc_ref, alc_ref, dtbp_ref, alp_ref, dsk_ref,
                wn_ref, ep_ref, el_ref, o_ref, st_ref, *, L, n_chunks, hpg, p):
    @pl.when(pl.program_id(2) == 0)
    def _():
        st_ref[...] = jnp.zeros_like(st_ref)

    npairs = hpg // 2
    a_c = -jnp.exp(alc_ref[0])
    a_p = -jnp.exp(alp_ref[0])
    tri = (lax.broadcasted_iota(jnp.int32, (L, L), 1) <= lax.broadcasted_iota(jnp.int32, (L, L), 0)).astype(BF16)
    r2 = lax.broadcasted_iota(jnp.int32, (2 * L, 2 * L), 0)
    c2 = lax.broadcasted_iota(jnp.int32, (2 * L, 2 * L), 1)
    bd = ((r2 <= c2) & ((r2 < L) == (c2 < L))).astype(BF16)
    lane2 = lax.broadcasted_iota(jnp.int32, (L, 2 * L), 1)
    row2 = lax.broadcasted_iota(jnp.int32, (L, 2 * L), 0)
    tri2 = jnp.where(lane2 < L, lane2, lane2 - L) <= row2
    lane_p = lax.broadcasted_iota(jnp.int32, (L, 2 * p), 1)

    def chunk(c, carry):
        r0 = pl.multiple_of(c * L, L)
        x = x_ref[pl.ds(r0, L), :].astype(F32)
        bmat = b_ref[pl.ds(r0, L), :]
        cmat = c_ref[pl.ds(r0, L), :]
        dt_c = _softplus(dtc_ref[0, pl.ds(r0, L), :] + dtbc_ref[0])
        cum_c = _dot_exact_lhs(tri, dt_c * a_c)
        p0 = pl.multiple_of(c * npairs, npairs)
        dt_p = _softplus(dtp_ref[0, pl.ds(p0, npairs), :] + dtbp_ref[0])
        cum_p = _dot_exact_rhs(dt_p * a_p, bd)
        both = _dot_exact_rhs(jnp.concatenate([cum_c, dt_c], axis=0), ep_ref[...])
        cum_e, dt_e = both[:L], both[L:]
        cum_l = cum_e if el_ref is None else _dot_exact_rhs(cum_c, el_ref[...])
        cb = lax.dot_general(cmat, bmat, (((1,), (1,)), ((), ())), preferred_element_type=F32)
        cb2 = jnp.concatenate([cb, cb], axis=1)
        x16 = x.astype(BF16)
        ys = []
        for j in range(npairs):
            seg = cum_l[:, j * 2 * L:(j + 1) * 2 * L] - cum_p[j:j + 1, :]
            w = jnp.exp(jnp.where(tri2, seg, NEG_BIG)) * cb2 * dt_p[j:j + 1, :]
            xp = x16[:, j * 2 * p:(j + 1) * 2 * p]
            rhs = jnp.concatenate([jnp.where(lane_p < p, xp, 0), jnp.where(lane_p >= p, xp, 0)], axis=0)
            ys.append(jnp.dot(w.astype(BF16), rhs, preferred_element_type=F32))
        y = jnp.concatenate(ys, axis=1)
        st = st_ref[...]
        y = y + jnp.dot(cmat, st.astype(BF16), preferred_element_type=F32) * jnp.exp(cum_e)
        cum_last = cum_e[L - 1:L, :]
        xw = (x * (jnp.exp(cum_last - cum_e) * dt_e)).astype(BF16)
        st_ref[...] = jnp.exp(cum_last) * st + jnp.dot(bmat.astype(F32).T.astype(BF16), xw,
                                                       preferred_element_type=F32)
        y = y + x * dsk_ref[0]
        zf = z_ref[pl.ds(r0, L), :].astype(F32)
        y = y * (zf * jax.nn.sigmoid(zf))
        y = y * lax.rsqrt(jnp.mean(y * y, axis=1, keepdims=True) + NORM_EPS) * wn_ref[0]
        o_ref[pl.ds(r0, L), :] = y.astype(o_ref.dtype)
        return carry

    lax.fori_loop(0, n_chunks, chunk, 0)


def _kernel_no_el(*refs, **kw):
    *ins, o_ref, st_ref = refs
    _ssd_kernel(*ins, None, o_ref, st_ref, **kw)


def ssd_core(proj, xbc, dt_raw, bsz, dt_bias, a_log, d_skip, w_norm, t_rows=256):
    m = proj.shape[0]
    s = m // bsz
    nh = dt_bias.shape[0]
    p, n, g, L = SSM_HEADDIM, SSM_STATE, SSM_GROUPS, SSM_CHUNK
    di = nh * p
    hpg = nh // g
    gw = hpg * p
    t_rows = min(t_rows, s)
    assert s % t_rows == 0 and t_rows % L == 0 and hpg % 2 == 0 and 2 * L == LANES
    nt = s // t_rows
    n_chunks = t_rows // L
    npairs = hpg // 2
    nc_all = m // L
    dt_c = dt_raw.reshape(m, g, hpg).transpose(1, 0, 2)
    dt_p = (dt_raw.reshape(nc_all, L, g, npairs, 2).transpose(2, 0, 3, 4, 1)
            .reshape(g, nc_all * npairs, 2 * L))
    col = lambda a: a.astype(F32).reshape(g, 1, hpg)
    pair = lambda a: jnp.broadcast_to(a.astype(F32).reshape(g, npairs, 2, 1), (g, npairs, 2, L)).reshape(g, npairs, 2 * L)
    expand = lambda rep: jnp.repeat(jnp.eye(hpg, dtype=BF16), rep, axis=1)
    d_e = jnp.repeat(d_skip.astype(F32), p).reshape(g, 1, gw)
    same = p == L
    row = lambda b, gg, t: b * nt + t
    gspec = lambda shape: pl.BlockSpec((1,) + shape, lambda b, gg, t: (gg, 0, 0))
    in_specs = [pl.BlockSpec((t_rows, gw), lambda b, gg, t: (row(b, gg, t), gg)),
                pl.BlockSpec((t_rows, n), lambda b, gg, t: (row(b, gg, t), di // n + gg)),
                pl.BlockSpec((t_rows, n), lambda b, gg, t: (row(b, gg, t), di // n + g + gg)),
                pl.BlockSpec((t_rows, gw), lambda b, gg, t: (row(b, gg, t), gg)),
                pl.BlockSpec((1, t_rows, hpg), lambda b, gg, t: (gg, row(b, gg, t), 0)),
                pl.BlockSpec((1, n_chunks * npairs, 2 * L), lambda b, gg, t: (gg, row(b, gg, t), 0)),
                gspec((1, hpg)), gspec((1, hpg)), gspec((npairs, 2 * L)), gspec((npairs, 2 * L)),
                gspec((1, gw)), gspec((1, gw)),
                pl.BlockSpec((hpg, gw), lambda b, gg, t: (0, 0))]
    args = [xbc, xbc, xbc, proj, dt_c, dt_p, col(dt_bias), col(a_log), pair(dt_bias), pair(a_log), d_e,
            w_norm.astype(F32).reshape(g, 1, gw), expand(p)]
    if not same:
        in_specs.append(pl.BlockSpec((hpg, hpg * L), lambda b, gg, t: (0, 0)))
        args.append(expand(L))
    kern = functools.partial(_ssd_kernel if not same else _kernel_no_el, L=L, n_chunks=n_chunks, hpg=hpg, p=p)
    return pl.pallas_call(
        kern,
        grid=(bsz, g, nt),
        in_specs=in_specs,
        out_specs=pl.BlockSpec((t_rows, gw), lambda b, gg, t: (row(b, gg, t), gg)),
        out_shape=jax.ShapeDtypeStruct((m, di), BF16),
        scratch_shapes=[pltpu.VMEM((n, gw), F32)],
        compiler_params=_cparams("parallel", "parallel", "arbitrary"),
        name="ssd_core",
    )(*args)


def mamba2_mixer(hn, bsz, w_in, conv_w, conv_b, dt_bias, a_log, d_skip, w_norm, w_out, x_res):
    nh = dt_bias.shape[0]
    di = nh * SSM_HEADDIM
    n_main = w_in.shape[1] - nh
    proj = matmul(hn, w_in[:, :n_main].astype(BF16), out_dtype=BF16)
    dt_raw = matmul(hn, w_in[:, n_main:].astype(BF16), tn=LANES)
    xbc = dwconv_silu(proj, di, conv_w, conv_b, bsz)
    y = ssd_core(proj, xbc, dt_raw, bsz, dt_bias, a_log, d_skip, w_norm)
    return matmul(y, w_out.astype(BF16), residual=x_res)


def _hgrn2_kernel(q_ref, f_ref, i_ref, g_ref, lb_ref, wn_ref, o_ref, st_ref, *, L, sub, n_chunks, layer):
    @pl.when(pl.program_id(2) == 0)
    def _():
        st_ref[...] = jnp.zeros_like(st_ref)

    dk = q_ref.shape[1]
    lbp = lb_ref[...]
    e = jnp.exp(lbp - jnp.max(lbp, axis=0, keepdims=True))
    sm = e / jnp.sum(e, axis=0, keepdims=True)
    lb = jnp.sum(sm[1:layer + 1], axis=0, keepdims=True) if layer > 0 else jnp.zeros((1, dk), F32)
    log_lb = jnp.log(lb)
    log_1mlb = jnp.log1p(-lb)
    tri = (lax.broadcasted_iota(jnp.int32, (L, L), 1) <= lax.broadcasted_iota(jnp.int32, (L, L), 0)).astype(BF16)
    sub_row = lax.broadcasted_iota(jnp.int32, (sub, dk), 0)
    lane_s = lax.broadcasted_iota(jnp.int32, (sub, sub), 1)
    nsub = L // sub

    def chunk(c, carry):
        r0 = pl.multiple_of(c * L, L)
        qf = q_ref[pl.ds(r0, L), :].astype(F32)
        qa = qf * jax.nn.sigmoid(qf)
        fx = f_ref[pl.ds(r0, L), :].astype(F32)
        key = (1.0 - lb) * jax.nn.sigmoid(-fx)
        log_f = jnp.logaddexp(log_lb, log_1mlb + jax.nn.log_sigmoid(fx))
        b = _dot_exact_lhs(tri, log_f)
        v = i_ref[pl.ds(r0, L), :]
        strips = []
        for blk in range(nsub):
            t0 = blk * sub
            bq = b[t0:t0 + sub]
            qb = qa[t0:t0 + sub]
            cols = []
            if blk > 0:
                ref_row = b[t0 - 1:t0]
                qd = (qb * jnp.exp(bq - ref_row)).astype(BF16)
                kd = (key[:t0] * jnp.exp(ref_row - b[:t0])).astype(BF16)
                cols.append(lax.dot_general(qd, kd, (((1,), (1,)), ((), ())), preferred_element_type=F32))
            diag = jnp.zeros((sub, sub), F32)
            for sl in range(sub):
                srow = t0 + sl
                dec = jnp.exp(jnp.where(sub_row >= sl, bq - b[srow:srow + 1], NEG_BIG))
                col = jnp.sum(qb * dec * key[srow:srow + 1], axis=1, keepdims=True)
                diag = jnp.where(lane_s == sl, col, diag)
            cols.append(diag)
            if t0 + sub < L:
                cols.append(jnp.zeros((sub, L - t0 - sub), F32))
            strips.append(jnp.concatenate(cols, axis=1))
        attn = jnp.concatenate(strips, axis=0)
        st = st_ref[...]
        o = (jnp.dot(attn.astype(BF16), v, preferred_element_type=F32)
             + lax.dot_general((qa * jnp.exp(b)).astype(BF16), st.astype(BF16), (((1,), (1,)), ((), ())),
                               preferred_element_type=F32))
        b_last = b[L - 1:L]
        kd = (key * jnp.exp(b_last - b)).astype(BF16)
        st_ref[...] = jnp.exp(b_last) * st + jnp.dot(v.astype(F32).T.astype(BF16), kd, preferred_element_type=F32)
        o = o * lax.rsqrt(jnp.mean(o * o, axis=1, keepdims=True) + NORM_EPS) * wn_ref[...]
        gf = g_ref[pl.ds(r0, L), :].astype(F32)
        o_ref[pl.ds(r0, L), :] = (o * (gf * jax.nn.sigmoid(gf))).astype(o_ref.dtype)
        return carry

    lax.fori_loop(0, n_chunks, chunk, 0)


def hgrn2_core(proj, bsz, lb_params, layer, w_gnorm, t_rows=512):
    m, d4 = proj.shape
    d = d4 // 4
    s = m // bsz
    dk, L = HGRN_DK, HGRN_CHUNK
    n_heads = d // dk
    assert d // n_heads == dk, "kernel assumes DV == DK"
    t_rows = min(t_rows, s)
    assert s % t_rows == 0 and t_rows % L == 0
    nt = s // t_rows
    depth = lb_params.shape[0]
    sec = lambda k: pl.BlockSpec((t_rows, dk), lambda b, h, t: (b * nt + t, k * n_heads + h))
    return pl.pallas_call(
        functools.partial(_hgrn2_kernel, L=L, sub=16, n_chunks=t_rows // L, layer=layer),
        grid=(bsz, n_heads, nt),
        in_specs=[sec(0), sec(1), sec(2), sec(3),
                  pl.BlockSpec((depth, dk), lambda b, h, t: (0, h)),
                  pl.BlockSpec((1, dk), lambda b, h, t: (0, 0))],
        out_specs=pl.BlockSpec((t_rows, dk), lambda b, h, t: (b * nt + t, h)),
        out_shape=jax.ShapeDtypeStruct((m, d), BF16),
        scratch_shapes=[pltpu.VMEM((dk, dk), F32)],
        compiler_params=_cparams("parallel", "parallel", "arbitrary"),
        name="hgrn2_core",
    )(proj, proj, proj, proj, lb_params.astype(F32), w_gnorm.reshape(1, dk).astype(F32))


def hgrn2_mixer(hn, bsz, w_in, lb_params, layer, w_gnorm, w_out, x_res):
    proj = matmul(hn, w_in.astype(BF16), out_dtype=BF16)
    o = hgrn2_core(proj, bsz, lb_params, layer, w_gnorm)
    return matmul(o, w_out.astype(BF16), residual=x_res)


def conv_ffn(x2d, bsz, g_norm, w_up, conv_w, conv_b, w_down):
    hn = rmsnorm(x2d, g_norm)
    u = matmul(hn, w_up.astype(BF16))
    a = ffn_conv_act(u, conv_w, conv_b, bsz)
    return matmul(a, w_down.astype(BF16), residual=x2d)


def kernel(x, norm_mix, norm_ffn, norm_final, mlstm_w_in, mlstm_b_gates, mlstm_w_norm, mlstm_w_out, diff_w_in, diff_lam_q1, diff_lam_k1, diff_lam_q2, diff_lam_k2, diff_w_subln, diff_w_out, ssm_w_in, ssm_conv_w, ssm_conv_b, ssm_dt_bias, ssm_a_log, ssm_d, ssm_w_norm, ssm_w_out, hgrn_w_in, hgrn_lb, hgrn_w_gnorm, hgrn_w_out, ffn_w_up, ffn_conv_w, ffn_conv_b, ffn_w_down):
    bsz, s, d = x.shape
    depth = norm_mix.shape[0]
    x2d = x.reshape(bsz * s, d)
    for layer in range(depth):
        kind, j = layer % N_MIXERS, layer // N_MIXERS
        hn = rmsnorm(x2d, norm_mix[layer])
        if kind == 0:
            x2d = mlstm_mixer(hn, bsz, mlstm_w_in[j], mlstm_b_gates[j], mlstm_w_norm[j], mlstm_w_out[j], x2d)
        elif kind == 1:
            lambda_init = 0.8 - 0.6 * math.exp(-0.3 * layer)
            x2d = diff_attn_mixer(hn, bsz, diff_w_in[j], diff_lam_q1[j], diff_lam_k1[j], diff_lam_q2[j],
                                  diff_lam_k2[j], diff_w_subln[j], diff_w_out[j], lambda_init, x2d)
        elif kind == 2:
            x2d = mamba2_mixer(hn, bsz, ssm_w_in[j], ssm_conv_w[j], ssm_conv_b[j], ssm_dt_bias[j],
                               ssm_a_log[j], ssm_d[j], ssm_w_norm[j], ssm_w_out[j], x2d)
        else:
            x2d = hgrn2_mixer(hn, bsz, hgrn_w_in[j], hgrn_lb, layer, hgrn_w_gnorm[j], hgrn_w_out[j], x2d)
        x2d = conv_ffn(x2d, bsz, norm_ffn[layer], ffn_w_up[layer], ffn_conv_w[layer], ffn_conv_b[layer],
                       ffn_w_down[layer])
    return rmsnorm(x2d, norm_final, out_dtype=x.dtype).reshape(bsz, s, d)
```

```python
import functools
import math

import jax
import jax.numpy as jnp
from jax import lax
from jax.experimental import pallas as pl
from jax.experimental.pallas import tpu as pltpu

NORM_EPS = 1e-6
N_MIXERS = 4
DIFF_DH = 128
SSM_HEADDIM = 64
SSM_STATE = 128
SSM_GROUPS = 8
SSM_CHUNK = 64
HGRN_DK = 128
HGRN_CHUNK = 64
LOG2E = 1.4426950408889634
NEG_BIG = -1e30

V7X_VMEM_BYTES = 64 * 1024 * 1024
VMEM_LIMIT_BYTES = 56 * 1024 * 1024
LANES = 128
SUBLANES = 8

F32 = jnp.float32
BF16 = jnp.bfloat16


def _cparams(*sem):
    return pltpu.CompilerParams(dimension_semantics=sem, vmem_limit_bytes=VMEM_LIMIT_BYTES)


def _mm_kernel(x_ref, w_ref, *rest, has_res, nk):
    if has_res:
        r_ref, o_ref = rest
    else:
        (o_ref,) = rest
    part = jnp.dot(x_ref[...], w_ref[...], preferred_element_type=F32)
    if nk == 1:
        if has_res:
            part = part + r_ref[...]
        o_ref[...] = part.astype(o_ref.dtype)
    else:
        k = pl.program_id(2)

        @pl.when(k == 0)
        def _():
            if has_res:
                o_ref[...] = r_ref[...] + part
            else:
                o_ref[...] = part

        @pl.when(k > 0)
        def _():
            o_ref[...] += part


def _pick_tile(n, target, quantum):
    best = None
    t = quantum
    while t <= min(n, target):
        if n % t == 0:
            best = t
        t += quantum
    assert best is not None, (n, target, quantum)
    return best


def matmul(x, w, residual=None, out_dtype=F32, tm=1024, tn=512, tk=4096):
    m, k = x.shape
    k2, n = w.shape
    assert k == k2
    tm = _pick_tile(m, tm, SUBLANES)
    tn = _pick_tile(n, tn, LANES)
    tk = k if k <= tk else _pick_tile(k, max(tk, 5504), LANES)
    nk = k // tk
    if nk > 1:
        assert out_dtype == F32
    has_res = residual is not None
    in_specs = [pl.BlockSpec((tm, tk), lambda i, j, kk: (i, kk)),
                pl.BlockSpec((tk, tn), lambda i, j, kk: (kk, j))]
    args = [x, w]
    if has_res:
        in_specs.append(pl.BlockSpec((tm, tn), lambda i, j, kk: (i, j)))
        args.append(residual)
    return pl.pallas_call(
        functools.partial(_mm_kernel, has_res=has_res, nk=nk),
        grid=(m // tm, n // tn, nk),
        in_specs=in_specs,
        out_specs=pl.BlockSpec((tm, tn), lambda i, j, kk: (i, j)),
        out_shape=jax.ShapeDtypeStruct((m, n), out_dtype),
        compiler_params=_cparams("parallel", "parallel", "arbitrary"),
        name="matmul",
    )(*args)


def _rmsnorm_kernel(x_ref, g_ref, o_ref):
    x = x_ref[...].astype(F32)
    ms = jnp.mean(x * x, axis=-1, keepdims=True)
    o_ref[...] = (x * lax.rsqrt(ms + NORM_EPS) * g_ref[...]).astype(o_ref.dtype)


def rmsnorm(x, g, out_dtype=BF16, tm=512):
    m, d = x.shape
    tm = _pick_tile(m, tm, SUBLANES)
    return pl.pallas_call(
        _rmsnorm_kernel,
        grid=(m // tm,),
        in_specs=[pl.BlockSpec((tm, d), lambda i: (i, 0)),
                  pl.BlockSpec((1, d), lambda i: (0, 0))],
        out_specs=pl.BlockSpec((tm, d), lambda i: (i, 0)),
        out_shape=jax.ShapeDtypeStruct((m, d), out_dtype),
        compiler_params=_cparams("parallel"),
        name="rmsnorm",
    )(x, g.reshape(1, d).astype(F32))


def _shift_rows(cur, prev_tail, shift):
    rolled = pltpu.roll(cur, shift, axis=0)
    tail = pltpu.roll(prev_tail, shift, axis=0)
    row = lax.broadcasted_iota(jnp.int32, (SUBLANES, cur.shape[1]), 0)
    head = jnp.where(row < shift, tail, rolled[:SUBLANES])
    return jnp.concatenate([head, rolled[SUBLANES:]], axis=0)


def _ffn_up_kernel(x_ref, w_ref, cw_ref, cb_ref, o_ref, tail_ref, *, tiles_per_seq):
    i = pl.program_id(0)
    j = pl.program_id(1)

    @pl.when((i == 0) & (j == 0))
    def _():
        tail_ref[...] = jnp.zeros_like(tail_ref)

    u = jnp.dot(x_ref[...], w_ref[...], preferred_element_type=F32)
    tail = jnp.where(i % tiles_per_seq == 0, 0.0, tail_ref[j])
    y = (cw_ref[2:3, :] * u + cw_ref[1:2, :] * _shift_rows(u, tail, 1)
         + cw_ref[0:1, :] * _shift_rows(u, tail, 2) + cb_ref[...])
    tail_ref[j] = u[-SUBLANES:]
    tn = o_ref.shape[1]
    g = y[:, :tn]
    o_ref[...] = (g * jax.nn.sigmoid(g) * y[:, tn:]).astype(o_ref.dtype)


def _pair_blocks(a, tn):
    lead = a.shape[:-1]
    f = a.shape[-1] // 2
    return a.reshape(*lead, 2, f // tn, tn).swapaxes(-3, -2).reshape(*lead, 2 * f)


def ffn_up_conv_act(hn, w_up, conv_w, conv_b, bsz, tm=1024, tn=256):
    m, k = hn.shape
    f = w_up.shape[1] // 2
    s = m // bsz
    tm = _pick_tile(s, tm, SUBLANES)
    tn = _pick_tile(f, tn, LANES)
    ncb = f // tn
    width = conv_w.shape[0]
    assert width == 3
    w = _pair_blocks(w_up, tn).astype(BF16)
    cw = _pair_blocks(conv_w.astype(F32), tn)
    cb = _pair_blocks(conv_b.reshape(1, 2 * f).astype(F32), tn)
    return pl.pallas_call(
        functools.partial(_ffn_up_kernel, tiles_per_seq=s // tm),
        grid=(m // tm, ncb),
        in_specs=[pl.BlockSpec((tm, k), lambda i, j: (i, 0)),
                  pl.BlockSpec((k, 2 * tn), lambda i, j: (0, j)),
                  pl.BlockSpec((width, 2 * tn), lambda i, j: (0, j)),
                  pl.BlockSpec((1, 2 * tn), lambda i, j: (0, j))],
        out_specs=pl.BlockSpec((tm, tn), lambda i, j: (i, j)),
        out_shape=jax.ShapeDtypeStruct((m, f), BF16),
        scratch_shapes=[pltpu.VMEM((ncb, SUBLANES, 2 * tn), F32)],
        compiler_params=_cparams("arbitrary", "arbitrary"),
        name="ffn_up_conv_act",
    )(hn, w, cw, cb)


def _lane_cumsum(x):
    lane = lax.broadcasted_iota(jnp.int32, x.shape, 1)
    shift = 1
    while shift < x.shape[1]:
        x = x + jnp.where(lane >= shift, pltpu.roll(x, shift, axis=1), 0.0)
        shift *= 2
    return x


def _mlstm_kernel(q_ref, k_ref, v_ref, og_ref, gi_ref, gf_ref, bi_ref, bf_ref, wn_ref, out_ref,
                  c_ref, n_ref, m_ref, *, n_chunks):
    L = LANES
    dk = q_ref.shape[1]
    dv = v_ref.shape[1]

    @pl.when(pl.program_id(2) == 0)
    def _():
        c_ref[...] = jnp.zeros_like(c_ref)
        n_ref[...] = jnp.zeros_like(n_ref)
        m_ref[...] = jnp.zeros_like(m_ref)

    row_i = lax.broadcasted_iota(jnp.int32, (L, L), 0)
    col_i = lax.broadcasted_iota(jnp.int32, (L, L), 1)
    causal = col_i <= row_i
    bias_i = bi_ref[0]
    bias_f = bf_ref[0]

    for c in range(n_chunks):
        rows = slice(c * L, (c + 1) * L)
        q = (q_ref[rows, :].astype(F32) * (dk ** -0.5)).astype(BF16)
        k = k_ref[rows, :]
        v = v_ref[rows, :]
        i_row = gi_ref[0, :, rows] + bias_i
        f_raw = gf_ref[0, :, rows] + bias_f
        lf_row = jnp.minimum(f_raw, 0.0) - jnp.log1p(jnp.exp(-jnp.abs(f_raw)))
        b_row = _lane_cumsum(lf_row)
        b_last = jnp.min(b_row, axis=1, keepdims=True)
        bm = jnp.broadcast_to(b_row, (L, L))
        im = jnp.broadcast_to(i_row, (L, L))
        bt = bm.T
        it = im.T
        m_prev = m_ref[0:1, :]
        d = jnp.where(causal, bt - bm + im, NEG_BIG)
        a = bt + m_prev
        m_t = jnp.maximum(a, jnp.max(d, axis=1, keepdims=True))
        w_intra = jnp.exp(d - m_t)
        w_inter = jnp.exp(a - m_t)
        sc = lax.dot_general(q, k, (((1,), (1,)), ((), ())), preferred_element_type=F32) * w_intra
        num = (jnp.dot(sc.astype(BF16), v, preferred_element_type=F32)
               + pltpu.repeat(w_inter, dv // LANES, axis=1)
               * jnp.dot(q, c_ref[...].astype(BF16), preferred_element_type=F32))
        qn = jnp.sum(q.astype(F32) * n_ref[0:1, :], axis=1, keepdims=True)
        den = jnp.sum(sc, axis=1, keepdims=True) + w_inter * qn
        den = jnp.maximum(jnp.abs(den), jnp.exp(-m_t))
        hc = num / pltpu.repeat(den, dv // LANES, axis=1)
        hc = hc * lax.rsqrt(jnp.mean(hc * hc, axis=1, keepdims=True) + NORM_EPS) * wn_ref[...]
        out_ref[rows, :] = (jax.nn.sigmoid(og_ref[rows, :].astype(F32)) * hc).astype(out_ref.dtype)

        g = b_last - bt + it
        m_new = jnp.maximum(b_last + m_prev, jnp.max(g, axis=0, keepdims=True))
        ws = jnp.exp(g - m_new)
        decay = jnp.exp(b_last + m_prev - m_new)
        kw = k.astype(F32) * pltpu.repeat(ws, dk // LANES, axis=1)
        c_ref[...] = (pltpu.repeat(decay, dv // LANES, axis=1) * c_ref[...]
                      + jnp.dot(kw.T.astype(BF16), v, preferred_element_type=F32))
        n_ref[...] = (pltpu.repeat(decay, dk // LANES, axis=1) * n_ref[...]
                      + jnp.sum(kw, axis=0, keepdims=True))
        m_ref[...] = jnp.broadcast_to(m_new, m_ref.shape)


def mlstm_core(proj, gates_t, bsz, b_gates, w_norm, t_rows=512):
    m = proj.shape[0]
    s = m // bsz
    n_heads = b_gates.shape[0] // 2
    dv = w_norm.shape[0] // n_heads
    dk = dv // 2
    t_rows = min(t_rows, s)
    assert s % t_rows == 0 and t_rows % LANES == 0 and dk % LANES == 0
    nt = s // t_rows
    bg = jnp.broadcast_to(b_gates.astype(F32)[:, None, None], (2 * n_heads, 1, LANES))
    row = lambda b, h, t: b * nt + t
    return pl.pallas_call(
        functools.partial(_mlstm_kernel, n_chunks=t_rows // LANES),
        grid=(bsz, n_heads, nt),
        in_specs=[pl.BlockSpec((t_rows, dk), lambda b, h, t: (row(b, h, t), h)),
                  pl.BlockSpec((t_rows, dk), lambda b, h, t: (row(b, h, t), n_heads + h)),
                  pl.BlockSpec((t_rows, dv), lambda b, h, t: (row(b, h, t), n_heads + h)),
                  pl.BlockSpec((t_rows, dv), lambda b, h, t: (row(b, h, t), 2 * n_heads + h)),
                  pl.BlockSpec((1, 1, t_rows), lambda b, h, t: (h, 0, row(b, h, t))),
                  pl.BlockSpec((1, 1, t_rows), lambda b, h, t: (n_heads + h, 0, row(b, h, t))),
                  pl.BlockSpec((1, 1, LANES), lambda b, h, t: (h, 0, 0)),
                  pl.BlockSpec((1, 1, LANES), lambda b, h, t: (n_heads + h, 0, 0)),
                  pl.BlockSpec((1, dv), lambda b, h, t: (0, h))],
        out_specs=pl.BlockSpec((t_rows, dv), lambda b, h, t: (row(b, h, t), h)),
        out_shape=jax.ShapeDtypeStruct((m, n_heads * dv), BF16),
        scratch_shapes=[pltpu.VMEM((dk, dv), F32), pltpu.VMEM((SUBLANES, dk), F32),
                        pltpu.VMEM((SUBLANES, LANES), F32)],
        compiler_params=_cparams("parallel", "parallel", "arbitrary"),
        name="mlstm_core",
    )(proj, proj, proj, proj, gates_t, gates_t, bg, bg, w_norm.reshape(1, n_heads * dv).astype(F32))


def mlstm_mixer(hn, bsz, w_in, b_gates, w_norm, w_out, x_res):
    n_heads = b_gates.shape[0] // 2
    n_main = w_in.shape[1] - 2 * n_heads
    w_g = jnp.pad(w_in[:, n_main:], ((0, 0), (0, LANES - 2 * n_heads))).astype(BF16)
    proj = matmul(hn, w_in[:, :n_main].astype(BF16), out_dtype=BF16)
    gates_t = matmul(hn, w_g, tn=LANES)[:, :2 * n_heads].T[:, None, :]
    hh = mlstm_core(proj, gates_t, bsz, b_gates, w_norm)
    return matmul(hh, w_out.astype(BF16), residual=x_res)


def _diff_attn_kernel(q_ref, k_ref, v_ref, sl_ref, lq1_ref, lk1_ref, lq2_ref, lk2_ref, g_ref, o_ref,
                      m_ref, l_ref, acc_ref, *, tq, tk, lambda_init):
    qi = pl.program_id(2)
    dh = DIFF_DH
    q = (q_ref[...].astype(F32) * (dh ** -0.5 * LOG2E)).astype(BF16)
    slope_row = pltpu.repeat(sl_ref[0], tk // LANES, axis=1)
    col = lax.broadcasted_iota(jnp.int32, (1, tk), 1).astype(F32)
    m_ref[...] = jnp.full(m_ref.shape, NEG_BIG, F32)
    l_ref[...] = jnp.zeros(l_ref.shape, F32)
    acc_ref[...] = jnp.zeros(acc_ref.shape, F32)

    def step(kk, masked):
        start = pl.multiple_of(kk * tk, tk)
        cb = (col + (kk * tk - qi * tq).astype(F32)) * slope_row
        kblk = k_ref[pl.ds(start, tk), :]
        vblk = v_ref[pl.ds(start, tk), :]
        for c in range(2):
            s = lax.dot_general(q[:, c * dh:(c + 1) * dh], kblk[:, c * dh:(c + 1) * dh],
                                (((1,), (1,)), ((), ())), preferred_element_type=F32)
            s = s + cb
            if masked:
                row_i = lax.broadcasted_iota(jnp.int32, (tq, tk), 0)
                col_i = lax.broadcasted_iota(jnp.int32, (tq, tk), 1)
                s = jnp.where(col_i <= row_i, s, NEG_BIG)
            m_prev = m_ref[c]
            m_new = jnp.maximum(m_prev, jnp.max(s, axis=1, keepdims=True))
            alpha = jnp.exp2(m_prev - m_new)
            p = jnp.exp2(s - pltpu.repeat(m_new, tk // LANES, axis=1))
            l_ref[c] = alpha * l_ref[c] + jnp.sum(p, axis=1, keepdims=True)
            m_ref[c] = m_new
            acc_ref[c] = (acc_ref[c] * pltpu.repeat(alpha, 2 * dh // LANES, axis=1)
                          + jnp.dot(p.astype(BF16), vblk, preferred_element_type=F32))

    def body(kk, carry):
        step(kk, False)
        return carry

    lax.fori_loop(0, qi, body, 0)
    step(qi, True)

    lam = (jnp.exp(jnp.sum(lq1_ref[...] * lk1_ref[...])) - jnp.exp(jnp.sum(lq2_ref[...] * lk2_ref[...]))
           + lambda_init)
    rep = 2 * dh // LANES
    o = (acc_ref[0] / pltpu.repeat(l_ref[0], rep, axis=1)
         - lam * (acc_ref[1] / pltpu.repeat(l_ref[1], rep, axis=1)))
    o = o * lax.rsqrt(jnp.mean(o * o, axis=-1, keepdims=True) + NORM_EPS) * g_ref[...]
    o_ref[...] = (o * (1.0 - lambda_init)).astype(o_ref.dtype)


def diff_attn_core(proj, bsz, lam_q1, lam_k1, lam_q2, lam_k2, w_subln, lambda_init, tq=512):
    m, d3 = proj.shape
    d = d3 // 3
    s = m // bsz
    hw = 2 * DIFF_DH
    n_heads = d // hw
    tq = min(tq, s)
    assert s % tq == 0
    nq = s // tq
    slopes = jnp.exp2(-8.0 * jnp.arange(1, n_heads + 1, dtype=F32) / n_heads) * LOG2E
    slopes = jnp.broadcast_to(slopes[:, None, None], (n_heads, 1, LANES))
    vec = lambda a: a.reshape(1, -1).astype(F32)
    small = pl.BlockSpec((1, DIFF_DH), lambda b, h, i: (0, 0))
    return pl.pallas_call(
        functools.partial(_diff_attn_kernel, tq=tq, tk=tq, lambda_init=lambda_init),
        grid=(bsz, n_heads, nq),
        in_specs=[pl.BlockSpec((tq, hw), lambda b, h, i: (b * nq + i, h)),
                  pl.BlockSpec((s, hw), lambda b, h, i: (b, n_heads + h)),
                  pl.BlockSpec((s, hw), lambda b, h, i: (b, 2 * n_heads + h)),
                  pl.BlockSpec((1, 1, LANES), lambda b, h, i: (h, 0, 0)),
                  small, small, small, small,
                  pl.BlockSpec((1, hw), lambda b, h, i: (0, 0))],
        out_specs=pl.BlockSpec((tq, hw), lambda b, h, i: (b * nq + i, h)),
        out_shape=jax.ShapeDtypeStruct((m, d), BF16),
        scratch_shapes=[pltpu.VMEM((2, tq, LANES), F32), pltpu.VMEM((2, tq, LANES), F32),
                        pltpu.VMEM((2, tq, hw), F32)],
        compiler_params=_cparams("parallel", "parallel", "arbitrary"),
        name="diff_attn",
    )(proj, proj, proj, slopes, vec(lam_q1), vec(lam_k1), vec(lam_q2), vec(lam_k2), vec(w_subln))


def diff_attn_mixer(hn, bsz, w_in, lam_q1, lam_k1, lam_q2, lam_k2, w_subln, w_out, lambda_init, x_res):
    proj = matmul(hn, w_in.astype(BF16), out_dtype=BF16)
    o = diff_attn_core(proj, bsz, lam_q1, lam_k1, lam_q2, lam_k2, w_subln, lambda_init)
    return matmul(o, w_out.astype(BF16), residual=x_res)


def _dwconv_silu_kernel(u_ref, w_ref, b_ref, o_ref, tail_ref, *, width):
    @pl.when(pl.program_id(2) == 0)
    def _():
        tail_ref[...] = jnp.zeros_like(tail_ref)

    u = u_ref[...].astype(F32)
    tail = tail_ref[...]
    y = w_ref[width - 1:width, :] * u + b_ref[...]
    for j in range(width - 1):
        y = y + w_ref[j:j + 1, :] * _shift_rows(u, tail, width - 1 - j)
    tail_ref[...] = u[-SUBLANES:]
    o_ref[...] = (y * jax.nn.sigmoid(y)).astype(o_ref.dtype)


def dwconv_silu(u, col0, conv_w, conv_b, bsz, ts=512, tc=512):
    m = u.shape[0]
    width, ch = conv_w.shape
    s = m // bsz
    ts = _pick_tile(s, ts, SUBLANES)
    tc = _pick_tile(math.gcd(ch, col0) if col0 else ch, tc, LANES)
    nsb = s // ts
    c0 = col0 // tc
    return pl.pallas_call(
        functools.partial(_dwconv_silu_kernel, width=width),
        grid=(ch // tc, bsz, nsb),
        in_specs=[pl.BlockSpec((ts, tc), lambda c, b, t: (b * nsb + t, c0 + c)),
                  pl.BlockSpec((width, tc), lambda c, b, t: (0, c)),
                  pl.BlockSpec((1, tc), lambda c, b, t: (0, c))],
        out_specs=pl.BlockSpec((ts, tc), lambda c, b, t: (b * nsb + t, c)),
        out_shape=jax.ShapeDtypeStruct((m, ch), BF16),
        scratch_shapes=[pltpu.VMEM((SUBLANES, tc), F32)],
        compiler_params=_cparams("parallel", "parallel", "arbitrary"),
        name="dwconv_silu",
    )(u, conv_w.astype(F32), conv_b.reshape(1, ch).astype(F32))


def _split3(v):
    hi = v.astype(BF16)
    r = v - hi.astype(F32)
    mid = r.astype(BF16)
    lo = (r - mid.astype(F32)).astype(BF16)
    return hi, mid, lo


def _dot_exact_rhs(lhs_f32, rhs_bf16):
    n = lhs_f32.shape[0]
    out = jnp.dot(jnp.concatenate(_split3(lhs_f32), axis=0), rhs_bf16, preferred_element_type=F32)
    return out[:n] + out[n:2 * n] + out[2 * n:]


def _dot_exact_lhs(lhs_bf16, rhs_f32):
    hi, mid, lo = _split3(rhs_f32)
    return (jnp.dot(lhs_bf16, hi, preferred_element_type=F32) + jnp.dot(lhs_bf16, mid, preferred_element_type=F32)
            + jnp.dot(lhs_bf16, lo, preferred_element_type=F32))


def _softplus(x):
    return jnp.maximum(x, 0.0) + jnp.log1p(jnp.exp(-jnp.abs(x)))


def _ssd_kernel(x_ref, b_ref, c_ref, z_ref, dtc_ref, dtp_ref, dtbc_ref, alc_ref, dtbp_ref, alp_ref, dsk_ref,
                wn_ref, ep_ref, el_ref, o_ref, st_ref, *, L, n_chunks, hpg, p):
    @pl.when(pl.program_id(2) == 0)
    def _():
        st_ref[...] = jnp.zeros_like(st_ref)

    npairs = hpg // 2
    a_c = -jnp.exp(alc_ref[0])
    a_p = -jnp.exp(alp_ref[0])
    tri = (lax.broadcasted_iota(jnp.int32, (L, L), 1) <= lax.broadcasted_iota(jnp.int32, (L, L), 0)).astype(BF16)
    r2 = lax.broadcasted_iota(jnp.int32, (2 * L, 2 * L), 0)
    c2 = lax.broadcasted_iota(jnp.int32, (2 * L, 2 * L), 1)
    bd = ((r2 <= c2) & ((r2 < L) == (c2 < L))).astype(BF16)
    lane2 = lax.broadcasted_iota(jnp.int32, (L, 2 * L), 1)
    row2 = lax.broadcasted_iota(jnp.int32, (L, 2 * L), 0)
    tri2 = jnp.where(lane2 < L, lane2, lane2 - L) <= row2
    lane_p = lax.broadcasted_iota(jnp.int32, (L, 2 * p), 1)

    def chunk(c, carry):
        r0 = pl.multiple_of(c * L, L)
        x = x_ref[pl.ds(r0, L), :].astype(F32)
        bmat = b_ref[pl.ds(r0, L), :]
        cmat = c_ref[pl.ds(r0, L), :]
        dt_c = _softplus(dtc_ref[0, pl.ds(r0, L), :] + dtbc_ref[0])
        cum_c = _dot_exact_lhs(tri, dt_c * a_c)
        p0 = pl.multiple_of(c * npairs, npairs)
        dt_p = _softplus(dtp_ref[0, pl.ds(p0, npairs), :] + dtbp_ref[0])
        cum_p = _dot_exact_rhs(dt_p * a_p, bd)
        both = _dot_exact_rhs(jnp.concatenate([cum_c, dt_c], axis=0), ep_ref[...])
        cum_e, dt_e = both[:L], both[L:]
        cum_l = cum_e if el_ref is None else _dot_exact_rhs(cum_c, el_ref[...])
        cb = lax.dot_general(cmat, bmat, (((1,), (1,)), ((), ())), preferred_element_type=F32)
        cb2 = jnp.concatenate([cb, cb], axis=1)
        x16 = x.astype(BF16)
        ys = []
        for j in range(npairs):
            seg = cum_l[:, j * 2 * L:(j + 1) * 2 * L] - cum_p[j:j + 1, :]
            w = jnp.exp(jnp.where(tri2, seg, NEG_BIG)) * cb2 * dt_p[j:j + 1, :]
            xp = x16[:, j * 2 * p:(j + 1) * 2 * p]
            rhs = jnp.concatenate([jnp.where(lane_p < p, xp, 0), jnp.where(lane_p >= p, xp, 0)], axis=0)
            ys.append(jnp.dot(w.astype(BF16), rhs, preferred_element_type=F32))
        y = jnp.concatenate(ys, axis=1)
        st = st_ref[...]
        y = y + jnp.dot(cmat, st.astype(BF16), preferred_element_type=F32) * jnp.exp(cum_e)
        cum_last = cum_e[L - 1:L, :]
        xw = (x * (jnp.exp(cum_last - cum_e) * dt_e)).astype(BF16)
        st_ref[...] = jnp.exp(cum_last) * st + jnp.dot(bmat.astype(F32).T.astype(BF16), xw,
                                                       preferred_element_type=F32)
        y = y + x * dsk_ref[0]
        zf = z_ref[pl.ds(r0, L), :].astype(F32)
        y = y * (zf * jax.nn.sigmoid(zf))
        y = y * lax.rsqrt(jnp.mean(y * y, axis=1, keepdims=True) + NORM_EPS) * wn_ref[0]
        o_ref[pl.ds(r0, L), :] = y.astype(o_ref.dtype)
        return carry

    lax.fori_loop(0, n_chunks, chunk, 0, unroll=4)


def _kernel_no_el(*refs, **kw):
    *ins, o_ref, st_ref = refs
    _ssd_kernel(*ins, None, o_ref, st_ref, **kw)


def ssd_core(proj, xbc, dt_raw, bsz, dt_bias, a_log, d_skip, w_norm, t_rows=256):
    m = proj.shape[0]
    s = m // bsz
    nh = dt_bias.shape[0]
    p, n, g, L = SSM_HEADDIM, SSM_STATE, SSM_GROUPS, SSM_CHUNK
    di = nh * p
    hpg = nh // g
    gw = hpg * p
    t_rows = min(t_rows, s)
    assert s % t_rows == 0 and t_rows % L == 0 and hpg % 2 == 0 and 2 * L == LANES
    nt = s // t_rows
    n_chunks = t_rows // L
    npairs = hpg // 2
    nc_all = m // L
    dt_c = dt_raw.reshape(m, g, hpg).transpose(1, 0, 2)
    dt_p = (dt_raw.reshape(nc_all, L, g, npairs, 2).transpose(2, 0, 3, 4, 1)
            .reshape(g, nc_all * npairs, 2 * L))
    col = lambda a: a.astype(F32).reshape(g, 1, hpg)
    pair = lambda a: jnp.broadcast_to(a.astype(F32).reshape(g, npairs, 2, 1), (g, npairs, 2, L)).reshape(g, npairs, 2 * L)
    expand = lambda rep: jnp.repeat(jnp.eye(hpg, dtype=BF16), rep, axis=1)
    d_e = jnp.repeat(d_skip.astype(F32), p).reshape(g, 1, gw)
    same = p == L
    row = lambda b, gg, t: b * nt + t
    gspec = lambda shape: pl.BlockSpec((1,) + shape, lambda b, gg, t: (gg, 0, 0))
    in_specs = [pl.BlockSpec((t_rows, gw), lambda b, gg, t: (row(b, gg, t), gg)),
                pl.BlockSpec((t_rows, n), lambda b, gg, t: (row(b, gg, t), di // n + gg)),
                pl.BlockSpec((t_rows, n), lambda b, gg, t: (row(b, gg, t), di // n + g + gg)),
                pl.BlockSpec((t_rows, gw), lambda b, gg, t: (row(b, gg, t), gg)),
                pl.BlockSpec((1, t_rows, hpg), lambda b, gg, t: (gg, row(b, gg, t), 0)),
                pl.BlockSpec((1, n_chunks * npairs, 2 * L), lambda b, gg, t: (gg, row(b, gg, t), 0)),
                gspec((1, hpg)), gspec((1, hpg)), gspec((npairs, 2 * L)), gspec((npairs, 2 * L)),
                gspec((1, gw)), gspec((1, gw)),
                pl.BlockSpec((hpg, gw), lambda b, gg, t: (0, 0))]
    args = [xbc, xbc, xbc, proj, dt_c, dt_p, col(dt_bias), col(a_log), pair(dt_bias), pair(a_log), d_e,
            w_norm.astype(F32).reshape(g, 1, gw), expand(p)]
    if not same:
        in_specs.append(pl.BlockSpec((hpg, hpg * L), lambda b, gg, t: (0, 0)))
        args.append(expand(L))
    kern = functools.partial(_ssd_kernel if not same else _kernel_no_el, L=L, n_chunks=n_chunks, hpg=hpg, p=p)
    return pl.pallas_call(
        kern,
        grid=(bsz, g, nt),
        in_specs=in_specs,
        out_specs=pl.BlockSpec((t_rows, gw), lambda b, gg, t: (row(b, gg, t), gg)),
        out_shape=jax.ShapeDtypeStruct((m, di), BF16),
        scratch_shapes=[pltpu.VMEM((n, gw), F32)],
        compiler_params=_cparams("parallel", "parallel", "arbitrary"),
        name="ssd_core",
    )(*args)


def mamba2_mixer(hn, bsz, w_in, conv_w, conv_b, dt_bias, a_log, d_skip, w_norm, w_out, x_res):
    nh = dt_bias.shape[0]
    di = nh * SSM_HEADDIM
    n_main = w_in.shape[1] - nh
    proj = matmul(hn, w_in[:, :n_main].astype(BF16), out_dtype=BF16)
    dt_raw = matmul(hn, w_in[:, n_main:].astype(BF16), tn=LANES)
    xbc = dwconv_silu(proj, di, conv_w, conv_b, bsz)
    y = ssd_core(proj, xbc, dt_raw, bsz, dt_bias, a_log, d_skip, w_norm)
    return matmul(y, w_out.astype(BF16), residual=x_res)


def _hgrn2_kernel(q_ref, f_ref, i_ref, g_ref, lb_ref, wn_ref, o_ref, st_ref, *, L, sub, n_chunks, layer):
    @pl.when(pl.program_id(2) == 0)
    def _():
        st_ref[...] = jnp.zeros_like(st_ref)

    dk = q_ref.shape[1]
    lbp = lb_ref[...]
    e = jnp.exp(lbp - jnp.max(lbp, axis=0, keepdims=True))
    sm = e / jnp.sum(e, axis=0, keepdims=True)
    lb = jnp.sum(sm[1:layer + 1], axis=0, keepdims=True) if layer > 0 else jnp.zeros((1, dk), F32)
    log_lb = jnp.log(lb)
    log_1mlb = jnp.log1p(-lb)
    tri = (lax.broadcasted_iota(jnp.int32, (L, L), 1) <= lax.broadcasted_iota(jnp.int32, (L, L), 0)).astype(BF16)
    sub_row = lax.broadcasted_iota(jnp.int32, (sub, dk), 0)
    lane_s = lax.broadcasted_iota(jnp.int32, (sub, sub), 1)
    nsub = L // sub

    def chunk(c, carry):
        r0 = pl.multiple_of(c * L, L)
        qf = q_ref[pl.ds(r0, L), :].astype(F32)
        qa = qf * jax.nn.sigmoid(qf)
        fx = f_ref[pl.ds(r0, L), :].astype(F32)
        key = (1.0 - lb) * jax.nn.sigmoid(-fx)
        log_f = jnp.logaddexp(log_lb, log_1mlb + jax.nn.log_sigmoid(fx))
        b = _dot_exact_lhs(tri, log_f)
        v = i_ref[pl.ds(r0, L), :]
        strips = []
        for blk in range(nsub):
            t0 = blk * sub
            bq = b[t0:t0 + sub]
            qb = qa[t0:t0 + sub]
            cols = []
            if blk > 0:
                ref_row = b[t0 - 1:t0]
                qd = (qb * jnp.exp(bq - ref_row)).astype(BF16)
                kd = (key[:t0] * jnp.exp(ref_row - b[:t0])).astype(BF16)
                cols.append(lax.dot_general(qd, kd, (((1,), (1,)), ((), ())), preferred_element_type=F32))
            diag = jnp.zeros((sub, sub), F32)
            for sl in range(sub):
                srow = t0 + sl
                dec = jnp.exp(jnp.where(sub_row >= sl, bq - b[srow:srow + 1], NEG_BIG))
                col = jnp.sum(qb * dec * key[srow:srow + 1], axis=1, keepdims=True)
                diag = jnp.where(lane_s == sl, col, diag)
            cols.append(diag)
            if t0 + sub < L:
                cols.append(jnp.zeros((sub, L - t0 - sub), F32))
            strips.append(jnp.concatenate(cols, axis=1))
        attn = jnp.concatenate(strips, axis=0)
        st = st_ref[...]
        o = (jnp.dot(attn.astype(BF16), v, preferred_element_type=F32)
             + lax.dot_general((qa * jnp.exp(b)).astype(BF16), st.astype(BF16), (((1,), (1,)), ((), ())),
                               preferred_element_type=F32))
        b_last = b[L - 1:L]
        kd = (key * jnp.exp(b_last - b)).astype(BF16)
        st_ref[...] = jnp.exp(b_last) * st + jnp.dot(v.astype(F32).T.astype(BF16), kd, preferred_element_type=F32)
        o = o * lax.rsqrt(jnp.mean(o * o, axis=1, keepdims=True) + NORM_EPS) * wn_ref[...]
        gf = g_ref[pl.ds(r0, L), :].astype(F32)
        o_ref[pl.ds(r0, L), :] = (o * (gf * jax.nn.sigmoid(gf))).astype(o_ref.dtype)
        return carry

    lax.fori_loop(0, n_chunks, chunk, 0, unroll=8)


def hgrn2_core(proj, bsz, lb_params, layer, w_gnorm, t_rows=512):
    m, d4 = proj.shape
    d = d4 // 4
    s = m // bsz
    dk, L = HGRN_DK, HGRN_CHUNK
    n_heads = d // dk
    assert d // n_heads == dk, "kernel assumes DV == DK"
    t_rows = min(t_rows, s)
    assert s % t_rows == 0 and t_rows % L == 0
    nt = s // t_rows
    depth = lb_params.shape[0]
    sec = lambda k: pl.BlockSpec((t_rows, dk), lambda b, h, t: (b * nt + t, k * n_heads + h))
    return pl.pallas_call(
        functools.partial(_hgrn2_kernel, L=L, sub=SUBLANES, n_chunks=t_rows // L, layer=layer),
        grid=(bsz, n_heads, nt),
        in_specs=[sec(0), sec(1), sec(2), sec(3),
                  pl.BlockSpec((depth, dk), lambda b, h, t: (0, h)),
                  pl.BlockSpec((1, dk), lambda b, h, t: (0, 0))],
        out_specs=pl.BlockSpec((t_rows, dk), lambda b, h, t: (b * nt + t, h)),
        out_shape=jax.ShapeDtypeStruct((m, d), BF16),
        scratch_shapes=[pltpu.VMEM((dk, dk), F32)],
        compiler_params=_cparams("parallel", "parallel", "arbitrary"),
        name="hgrn2_core",
    )(proj, proj, proj, proj, lb_params.astype(F32), w_gnorm.reshape(1, dk).astype(F32))


def hgrn2_mixer(hn, bsz, w_in, lb_params, layer, w_gnorm, w_out, x_res):
    proj = matmul(hn, w_in.astype(BF16), out_dtype=BF16)
    o = hgrn2_core(proj, bsz, lb_params, layer, w_gnorm)
    return matmul(o, w_out.astype(BF16), residual=x_res)


def conv_ffn(x2d, bsz, g_norm, w_up, conv_w, conv_b, w_down):
    hn = rmsnorm(x2d, g_norm)
    a = ffn_up_conv_act(hn, w_up, conv_w, conv_b, bsz)
    return matmul(a, w_down.astype(BF16), residual=x2d)


def kernel(x, norm_mix, norm_ffn, norm_final, mlstm_w_in, mlstm_b_gates, mlstm_w_norm, mlstm_w_out, diff_w_in, diff_lam_q1, diff_lam_k1, diff_lam_q2, diff_lam_k2, diff_w_subln, diff_w_out, ssm_w_in, ssm_conv_w, ssm_conv_b, ssm_dt_bias, ssm_a_log, ssm_d, ssm_w_norm, ssm_w_out, hgrn_w_in, hgrn_lb, hgrn_w_gnorm, hgrn_w_out, ffn_w_up, ffn_conv_w, ffn_conv_b, ffn_w_down):
    bsz, s, d = x.shape
    depth = norm_mix.shape[0]
    x2d = x.reshape(bsz * s, d)
    for layer in range(depth):
        kind, j = layer % N_MIXERS, layer // N_MIXERS
        hn = rmsnorm(x2d, norm_mix[layer])
        if kind == 0:
            x2d = mlstm_mixer(hn, bsz, mlstm_w_in[j], mlstm_b_gates[j], mlstm_w_norm[j], mlstm_w_out[j], x2d)
        elif kind == 1:
            lambda_init = 0.8 - 0.6 * math.exp(-0.3 * layer)
            x2d = diff_attn_mixer(hn, bsz, diff_w_in[j], diff_lam_q1[j], diff_lam_k1[j], diff_lam_q2[j],
                                  diff_lam_k2[j], diff_w_subln[j], diff_w_out[j], lambda_init, x2d)
        elif kind == 2:
            x2d = mamba2_mixer(hn, bsz, ssm_w_in[j], ssm_conv_w[j], ssm_conv_b[j], ssm_dt_bias[j],
                               ssm_a_log[j], ssm_d[j], ssm_w_norm[j], ssm_w_out[j], x2d)
        else:
            x2d = hgrn2_mixer(hn, bsz, hgrn_w_in[j], hgrn_lb, layer, hgrn_w_gnorm[j], hgrn_w_out[j], x2d)
        x2d = conv_ffn(x2d, bsz, norm_ffn[layer], ffn_w_up[layer], ffn_conv_w[layer], ffn_conv_b[layer],
                       ffn_w_down[layer])
    return rmsnorm(x2d, norm_final, out_dtype=x.dtype).reshape(bsz, s, d)
```

```python
import functools
import math

import jax
import jax.numpy as jnp
from jax import lax
from jax.experimental import pallas as pl
from jax.experimental.pallas import tpu as pltpu

NORM_EPS = 1e-6
N_MIXERS = 4
DIFF_DH = 128
SSM_HEADDIM = 64
SSM_STATE = 128
SSM_GROUPS = 8
SSM_CHUNK = 64
HGRN_DK = 128
HGRN_CHUNK = 64
LOG2E = 1.4426950408889634
NEG_BIG = -1e30

V7X_VMEM_BYTES = 64 * 1024 * 1024
VMEM_LIMIT_BYTES = 56 * 1024 * 1024
LANES = 128
SUBLANES = 8

F32 = jnp.float32
BF16 = jnp.bfloat16


def _cparams(*sem):
    return pltpu.CompilerParams(dimension_semantics=sem, vmem_limit_bytes=VMEM_LIMIT_BYTES)


def _lane_tile(x, n):
    return x if n == 1 else jnp.concatenate([x] * n, axis=1)


def _mm_kernel(x_ref, w_ref, *rest, has_res, nk):
    if has_res:
        r_ref, o_ref = rest
    else:
        (o_ref,) = rest
    part = jnp.dot(x_ref[...], w_ref[...], preferred_element_type=F32)
    if nk == 1:
        if has_res:
            part = part + r_ref[...]
        o_ref[...] = part.astype(o_ref.dtype)
    else:
        k = pl.program_id(2)

        @pl.when(k == 0)
        def _():
            if has_res:
                o_ref[...] = r_ref[...] + part
            else:
                o_ref[...] = part

        @pl.when(k > 0)
        def _():
            o_ref[...] += part


def _pick_tile(n, target, quantum):
    best = None
    t = quantum
    while t <= min(n, target):
        if n % t == 0:
            best = t
        t += quantum
    assert best is not None, (n, target, quantum)
    return best


def matmul(x, w, residual=None, out_dtype=F32, tm=1024, tn=512, tk=4096):
    m, k = x.shape
    k2, n = w.shape
    assert k == k2
    tm = _pick_tile(m, tm, SUBLANES)
    tn = _pick_tile(n, tn, LANES)
    tk = k if k <= tk else _pick_tile(k, max(tk, 5504), LANES)
    nk = k // tk
    if nk > 1:
        assert out_dtype == F32
    has_res = residual is not None
    in_specs = [pl.BlockSpec((tm, tk), lambda i, j, kk: (i, kk)),
                pl.BlockSpec((tk, tn), lambda i, j, kk: (kk, j))]
    args = [x, w]
    if has_res:
        in_specs.append(pl.BlockSpec((tm, tn), lambda i, j, kk: (i, j)))
        args.append(residual)
    return pl.pallas_call(
        functools.partial(_mm_kernel, has_res=has_res, nk=nk),
        grid=(m // tm, n // tn, nk),
        in_specs=in_specs,
        out_specs=pl.BlockSpec((tm, tn), lambda i, j, kk: (i, j)),
        out_shape=jax.ShapeDtypeStruct((m, n), out_dtype),
        compiler_params=_cparams("parallel", "parallel", "arbitrary"),
        name="matmul",
    )(*args)


def _rmsnorm_kernel(x_ref, g_ref, o_ref):
    x = x_ref[...].astype(F32)
    ms = jnp.mean(x * x, axis=-1, keepdims=True)
    o_ref[...] = (x * lax.rsqrt(ms + NORM_EPS) * g_ref[...]).astype(o_ref.dtype)


def rmsnorm(x, g, out_dtype=BF16, tm=512):
    m, d = x.shape
    tm = _pick_tile(m, tm, SUBLANES)
    return pl.pallas_call(
        _rmsnorm_kernel,
        grid=(m // tm,),
        in_specs=[pl.BlockSpec((tm, d), lambda i: (i, 0)),
                  pl.BlockSpec((1, d), lambda i: (0, 0))],
        out_specs=pl.BlockSpec((tm, d), lambda i: (i, 0)),
        out_shape=jax.ShapeDtypeStruct((m, d), out_dtype),
        compiler_params=_cparams("parallel"),
        name="rmsnorm",
    )(x, g.reshape(1, d).astype(F32))


def _shift_rows(cur, prev_tail, shift):
    rolled = pltpu.roll(cur, shift, axis=0)
    tail = pltpu.roll(prev_tail, shift, axis=0)
    row = lax.broadcasted_iota(jnp.int32, (SUBLANES, cur.shape[1]), 0)
    head = jnp.where(row < shift, tail, rolled[:SUBLANES])
    return jnp.concatenate([head, rolled[SUBLANES:]], axis=0)


def _ffn_up_kernel(x_ref, wg_ref, wv_ref, cg_ref, cv_ref, bg_ref, bv_ref, o_ref,
                   ug_ref, uv_ref, tg_ref, tv_ref, *, tiles_per_seq, ncb):
    i = pl.program_id(0)
    j = pl.program_id(1)

    @pl.when((i == 0) & (j == 0))
    def _():
        for r in (ug_ref, uv_ref, tg_ref, tv_ref):
            r[...] = jnp.zeros_like(r)

    x = x_ref[...]
    ug_new = jnp.dot(x, wg_ref[...], preferred_element_type=F32)
    uv_new = jnp.dot(x, wv_ref[...], preferred_element_type=F32)
    jm = jnp.maximum(j - 1, 0)
    seq_start = i % tiles_per_seq == 0

    def conv(u, t_ref, w_ref, b_ref):
        old = t_ref[jm]
        tail = jnp.where(seq_start, 0.0, old)
        y = (w_ref[2:3, :] * u + w_ref[1:2, :] * _shift_rows(u, tail, 1)
             + w_ref[0:1, :] * _shift_rows(u, tail, 2) + b_ref[...])
        t_ref[jm] = jnp.where(j > 0, u[-SUBLANES:], old)
        return y

    g = conv(ug_ref[...], tg_ref, cg_ref, bg_ref)
    v = conv(uv_ref[...], tv_ref, cv_ref, bv_ref)
    o_ref[...] = (g * jax.nn.sigmoid(g) * v).astype(o_ref.dtype)
    ug_ref[...] = ug_new
    uv_ref[...] = uv_new


def ffn_up_conv_act(hn, w_up, conv_w, conv_b, bsz, tm=1024, tn=256):
    m, k = hn.shape
    f = w_up.shape[1] // 2
    s = m // bsz
    tm = _pick_tile(s, tm, SUBLANES)
    tn = _pick_tile(f, tn, LANES)
    ncb = f // tn
    width = conv_w.shape[0]
    assert width == 3
    cw = conv_w.astype(F32)
    cb = conv_b.reshape(1, 2 * f).astype(F32)
    cur = lambda i, j: (0, jnp.minimum(j, ncb - 1))
    cur_v = lambda i, j: (0, jnp.minimum(j, ncb - 1) + ncb)
    prev = lambda i, j: (0, jnp.maximum(j - 1, 0))
    prev_v = lambda i, j: (0, jnp.maximum(j - 1, 0) + ncb)
    return pl.pallas_call(
        functools.partial(_ffn_up_kernel, tiles_per_seq=s // tm, ncb=ncb),
        grid=(m // tm, ncb + 1),
        in_specs=[pl.BlockSpec((tm, k), lambda i, j: (i, 0)),
                  pl.BlockSpec((k, tn), cur), pl.BlockSpec((k, tn), cur_v),
                  pl.BlockSpec((width, tn), prev), pl.BlockSpec((width, tn), prev_v),
                  pl.BlockSpec((1, tn), prev), pl.BlockSpec((1, tn), prev_v)],
        out_specs=pl.BlockSpec((tm, tn), lambda i, j: (i, jnp.maximum(j - 1, 0))),
        out_shape=jax.ShapeDtypeStruct((m, f), BF16),
        scratch_shapes=[pltpu.VMEM((tm, tn), F32), pltpu.VMEM((tm, tn), F32),
                        pltpu.VMEM((ncb, SUBLANES, tn), F32), pltpu.VMEM((ncb, SUBLANES, tn), F32)],
        compiler_params=_cparams("arbitrary", "arbitrary"),
        name="ffn_up_conv_act",
    )(hn, w_up, w_up, cw, cw, cb, cb)


def _lane_cumsum(x):
    lane = lax.broadcasted_iota(jnp.int32, x.shape, 1)
    shift = 1
    while shift < x.shape[1]:
        x = x + jnp.where(lane >= shift, pltpu.roll(x, shift, axis=1), 0.0)
        shift *= 2
    return x


def _mlstm_kernel(q_ref, k_ref, v_ref, og_ref, gi_ref, gf_ref, bi_ref, bf_ref, wn_ref, out_ref,
                  c_ref, n_ref, m_ref, *, n_chunks):
    L = LANES
    dk = q_ref.shape[1]
    dv = v_ref.shape[1]

    @pl.when(pl.program_id(2) == 0)
    def _():
        c_ref[...] = jnp.zeros_like(c_ref)
        n_ref[...] = jnp.zeros_like(n_ref)
        m_ref[...] = jnp.zeros_like(m_ref)

    row_i = lax.broadcasted_iota(jnp.int32, (L, L), 0)
    col_i = lax.broadcasted_iota(jnp.int32, (L, L), 1)
    causal = col_i <= row_i
    bias_i = bi_ref[0]
    bias_f = bf_ref[0]

    for c in range(n_chunks):
        rows = slice(c * L, (c + 1) * L)
        q = (q_ref[rows, :].astype(F32) * (dk ** -0.5)).astype(BF16)
        k = k_ref[rows, :]
        v = v_ref[rows, :]
        i_row = gi_ref[0, :, rows] + bias_i
        f_raw = gf_ref[0, :, rows] + bias_f
        lf_row = jnp.minimum(f_raw, 0.0) - jnp.log1p(jnp.exp(-jnp.abs(f_raw)))
        b_row = _lane_cumsum(lf_row)
        b_last = jnp.min(b_row, axis=1, keepdims=True)
        bm = jnp.broadcast_to(b_row, (L, L))
        im = jnp.broadcast_to(i_row, (L, L))
        bt = bm.T
        it = im.T
        m_prev = m_ref[0:1, :]
        d = jnp.where(causal, bt - bm + im, NEG_BIG)
        a = bt + m_prev
        m_t = jnp.maximum(a, jnp.max(d, axis=1, keepdims=True))
        w_intra = jnp.exp(d - m_t)
        w_inter = jnp.exp(a - m_t)
        sc = lax.dot_general(q, k, (((1,), (1,)), ((), ())), preferred_element_type=F32) * w_intra
        num = (jnp.dot(sc.astype(BF16), v, preferred_element_type=F32)
               + _lane_tile(w_inter, dv // LANES)
               * jnp.dot(q, c_ref[...].astype(BF16), preferred_element_type=F32))
        qn = jnp.sum(q.astype(F32) * n_ref[0:1, :], axis=1, keepdims=True)
        den = jnp.sum(sc, axis=1, keepdims=True) + w_inter * qn
        den = jnp.maximum(jnp.abs(den), jnp.exp(-m_t))
        hc = num / _lane_tile(den, dv // LANES)
        hc = hc * lax.rsqrt(jnp.mean(hc * hc, axis=1, keepdims=True) + NORM_EPS) * wn_ref[...]
        out_ref[rows, :] = (jax.nn.sigmoid(og_ref[rows, :].astype(F32)) * hc).astype(out_ref.dtype)

        g = b_last - bt + it
        m_new = jnp.maximum(b_last + m_prev, jnp.max(g, axis=0, keepdims=True))
        ws = jnp.exp(g - m_new)
        decay = jnp.exp(b_last + m_prev - m_new)
        kw = k.astype(F32) * _lane_tile(ws, dk // LANES)
        c_ref[...] = (_lane_tile(decay, dv // LANES) * c_ref[...]
                      + jnp.dot(kw.T.astype(BF16), v, preferred_element_type=F32))
        n_ref[...] = (_lane_tile(decay, dk // LANES) * n_ref[...]
                      + jnp.sum(kw, axis=0, keepdims=True))
        m_ref[...] = jnp.broadcast_to(m_new, m_ref.shape)


def mlstm_core(proj, gates_t, bsz, b_gates, w_norm, t_rows=512):
    m = proj.shape[0]
    s = m // bsz
    n_heads = b_gates.shape[0] // 2
    dv = w_norm.shape[0] // n_heads
    dk = dv // 2
    t_rows = min(t_rows, s)
    assert s % t_rows == 0 and t_rows % LANES == 0 and dk % LANES == 0
    nt = s // t_rows
    bg = jnp.broadcast_to(b_gates.astype(F32)[:, None, None], (2 * n_heads, 1, LANES))
    row = lambda b, h, t: b * nt + t
    return pl.pallas_call(
        functools.partial(_mlstm_kernel, n_chunks=t_rows // LANES),
        grid=(bsz, n_heads, nt),
        in_specs=[pl.BlockSpec((t_rows, dk), lambda b, h, t: (row(b, h, t), h)),
                  pl.BlockSpec((t_rows, dk), lambda b, h, t: (row(b, h, t), n_heads + h)),
                  pl.BlockSpec((t_rows, dv), lambda b, h, t: (row(b, h, t), n_heads + h)),
                  pl.BlockSpec((t_rows, dv), lambda b, h, t: (row(b, h, t), 2 * n_heads + h)),
                  pl.BlockSpec((1, 1, t_rows), lambda b, h, t: (h, 0, row(b, h, t))),
                  pl.BlockSpec((1, 1, t_rows), lambda b, h, t: (n_heads + h, 0, row(b, h, t))),
                  pl.BlockSpec((1, 1, LANES), lambda b, h, t: (h, 0, 0)),
                  pl.BlockSpec((1, 1, LANES), lambda b, h, t: (n_heads + h, 0, 0)),
                  pl.BlockSpec((1, dv), lambda b, h, t: (0, h))],
        out_specs=pl.BlockSpec((t_rows, dv), lambda b, h, t: (row(b, h, t), h)),
        out_shape=jax.ShapeDtypeStruct((m, n_heads * dv), BF16),
        scratch_shapes=[pltpu.VMEM((dk, dv), F32), pltpu.VMEM((SUBLANES, dk), F32),
                        pltpu.VMEM((SUBLANES, LANES), F32)],
        compiler_params=_cparams("parallel", "parallel", "arbitrary"),
        name="mlstm_core",
    )(proj, proj, proj, proj, gates_t, gates_t, bg, bg, w_norm.reshape(1, n_heads * dv).astype(F32))


def mlstm_mixer(hn, bsz, w_in, b_gates, w_norm, w_out, x_res):
    n_heads = b_gates.shape[0] // 2
    n_main = w_in.shape[1] - 2 * n_heads
    w_g = jnp.pad(w_in[:, n_main:], ((0, 0), (0, LANES - 2 * n_heads))).astype(BF16)
    proj = matmul(hn, w_in[:, :n_main].astype(BF16), out_dtype=BF16)
    gates_t = matmul(hn, w_g, tn=LANES)[:, :2 * n_heads].T[:, None, :]
    hh = mlstm_core(proj, gates_t, bsz, b_gates, w_norm)
    return matmul(hh, w_out.astype(BF16), residual=x_res)


def _diff_attn_kernel(q_ref, k_ref, v_ref, sl_ref, lq1_ref, lk1_ref, lq2_ref, lk2_ref, g_ref, o_ref,
                      m_ref, l_ref, acc_ref, *, tq, tk, lambda_init):
    qi = pl.program_id(2)
    dh = DIFF_DH
    q = (q_ref[...].astype(F32) * (dh ** -0.5 * LOG2E)).astype(BF16)
    slope_row = _lane_tile(sl_ref[0], tk // LANES)
    col = lax.broadcasted_iota(jnp.int32, (1, tk), 1).astype(F32)
    m_ref[...] = jnp.full(m_ref.shape, NEG_BIG, F32)
    l_ref[...] = jnp.zeros(l_ref.shape, F32)
    acc_ref[...] = jnp.zeros(acc_ref.shape, F32)

    def step(kk, masked):
        start = pl.multiple_of(kk * tk, tk)
        cb = (col + (kk * tk - qi * tq).astype(F32)) * slope_row
        kblk = k_ref[pl.ds(start, tk), :]
        vblk = v_ref[pl.ds(start, tk), :]
        for c in range(2):
            s = lax.dot_general(q[:, c * dh:(c + 1) * dh], kblk[:, c * dh:(c + 1) * dh],
                                (((1,), (1,)), ((), ())), preferred_element_type=F32)
            s = s + cb
            if masked:
                row_i = lax.broadcasted_iota(jnp.int32, (tq, tk), 0)
                col_i = lax.broadcasted_iota(jnp.int32, (tq, tk), 1)
                s = jnp.where(col_i <= row_i, s, NEG_BIG)
            m_prev = m_ref[c]
            m_new = jnp.maximum(m_prev, jnp.max(s, axis=1, keepdims=True))
            alpha = jnp.exp2(m_prev - m_new)
            p = jnp.exp2(s - _lane_tile(m_new, tk // LANES))
            l_ref[c] = alpha * l_ref[c] + jnp.sum(p, axis=1, keepdims=True)
            m_ref[c] = m_new
            acc_ref[c] = (acc_ref[c] * _lane_tile(alpha, 2 * dh // LANES)
                          + jnp.dot(p.astype(BF16), vblk, preferred_element_type=F32))

    def body(kk, carry):
        step(kk, False)
        return carry

    lax.fori_loop(0, qi, body, 0)
    step(qi, True)

    lam = (jnp.exp(jnp.sum(lq1_ref[...] * lk1_ref[...])) - jnp.exp(jnp.sum(lq2_ref[...] * lk2_ref[...]))
           + lambda_init)
    rep = 2 * dh // LANES
    o = (acc_ref[0] / _lane_tile(l_ref[0], rep)
         - lam * (acc_ref[1] / _lane_tile(l_ref[1], rep)))
    o = o * lax.rsqrt(jnp.mean(o * o, axis=-1, keepdims=True) + NORM_EPS) * g_ref[...]
    o_ref[...] = (o * (1.0 - lambda_init)).astype(o_ref.dtype)


def diff_attn_core(proj, bsz, lam_q1, lam_k1, lam_q2, lam_k2, w_subln, lambda_init, tq=512):
    m, d3 = proj.shape
    d = d3 // 3
    s = m // bsz
    hw = 2 * DIFF_DH
    n_heads = d // hw
    tq = min(tq, s)
    assert s % tq == 0
    nq = s // tq
    slopes = jnp.exp2(-8.0 * jnp.arange(1, n_heads + 1, dtype=F32) / n_heads) * LOG2E
    slopes = jnp.broadcast_to(slopes[:, None, None], (n_heads, 1, LANES))
    vec = lambda a: a.reshape(1, -1).astype(F32)
    small = pl.BlockSpec((1, DIFF_DH), lambda b, h, i: (0, 0))
    return pl.pallas_call(
        functools.partial(_diff_attn_kernel, tq=tq, tk=tq, lambda_init=lambda_init),
        grid=(bsz, n_heads, nq),
        in_specs=[pl.BlockSpec((tq, hw), lambda b, h, i: (b * nq + i, h)),
                  pl.BlockSpec((s, hw), lambda b, h, i: (b, n_heads + h)),
                  pl.BlockSpec((s, hw), lambda b, h, i: (b, 2 * n_heads + h)),
                  pl.BlockSpec((1, 1, LANES), lambda b, h, i: (h, 0, 0)),
                  small, small, small, small,
                  pl.BlockSpec((1, hw), lambda b, h, i: (0, 0))],
        out_specs=pl.BlockSpec((tq, hw), lambda b, h, i: (b * nq + i, h)),
        out_shape=jax.ShapeDtypeStruct((m, d), BF16),
        scratch_shapes=[pltpu.VMEM((2, tq, LANES), F32), pltpu.VMEM((2, tq, LANES), F32),
                        pltpu.VMEM((2, tq, hw), F32)],
        compiler_params=_cparams("parallel", "parallel", "arbitrary"),
        name="diff_attn",
    )(proj, proj, proj, slopes, vec(lam_q1), vec(lam_k1), vec(lam_q2), vec(lam_k2), vec(w_subln))


def diff_attn_mixer(hn, bsz, w_in, lam_q1, lam_k1, lam_q2, lam_k2, w_subln, w_out, lambda_init, x_res):
    proj = matmul(hn, w_in.astype(BF16), out_dtype=BF16)
    o = diff_attn_core(proj, bsz, lam_q1, lam_k1, lam_q2, lam_k2, w_subln, lambda_init)
    return matmul(o, w_out.astype(BF16), residual=x_res)


def _dwconv_silu_kernel(u_ref, w_ref, b_ref, o_ref, tail_ref, *, width):
    @pl.when(pl.program_id(2) == 0)
    def _():
        tail_ref[...] = jnp.zeros_like(tail_ref)

    u = u_ref[...].astype(F32)
    tail = tail_ref[...]
    y = w_ref[width - 1:width, :] * u + b_ref[...]
    for j in range(width - 1):
        y = y + w_ref[j:j + 1, :] * _shift_rows(u, tail, width - 1 - j)
    tail_ref[...] = u[-SUBLANES:]
    o_ref[...] = (y * jax.nn.sigmoid(y)).astype(o_ref.dtype)


def dwconv_silu(u, col0, conv_w, conv_b, bsz, ts=512, tc=512):
    m = u.shape[0]
    width, ch = conv_w.shape
    s = m // bsz
    ts = _pick_tile(s, ts, SUBLANES)
    tc = _pick_tile(math.gcd(ch, col0) if col0 else ch, tc, LANES)
    nsb = s // ts
    c0 = col0 // tc
    return pl.pallas_call(
        functools.partial(_dwconv_silu_kernel, width=width),
        grid=(ch // tc, bsz, nsb),
        in_specs=[pl.BlockSpec((ts, tc), lambda c, b, t: (b * nsb + t, c0 + c)),
                  pl.BlockSpec((width, tc), lambda c, b, t: (0, c)),
                  pl.BlockSpec((1, tc), lambda c, b, t: (0, c))],
        out_specs=pl.BlockSpec((ts, tc), lambda c, b, t: (b * nsb + t, c)),
        out_shape=jax.ShapeDtypeStruct((m, ch), BF16),
        scratch_shapes=[pltpu.VMEM((SUBLANES, tc), F32)],
        compiler_params=_cparams("parallel", "parallel", "arbitrary"),
        name="dwconv_silu",
    )(u, conv_w.astype(F32), conv_b.reshape(1, ch).astype(F32))


def _split3(v):
    hi = v.astype(BF16)
    r = v - hi.astype(F32)
    mid = r.astype(BF16)
    lo = (r - mid.astype(F32)).astype(BF16)
    return hi, mid, lo


def _dot_exact_rhs(lhs_f32, rhs_bf16):
    n = lhs_f32.shape[0]
    out = jnp.dot(jnp.concatenate(_split3(lhs_f32), axis=0), rhs_bf16, preferred_element_type=F32)
    return out[:n] + out[n:2 * n] + out[2 * n:]


def _dot_exact_lhs(lhs_bf16, rhs_f32):
    hi, mid, lo = _split3(rhs_f32)
    return (jnp.dot(lhs_bf16, hi, preferred_element_type=F32) + jnp.dot(lhs_bf16, mid, preferred_element_type=F32)
            + jnp.dot(lhs_bf16, lo, preferred_element_type=F32))


def _softplus(x):
    return jnp.maximum(x, 0.0) + jnp.log1p(jnp.exp(-jnp.abs(x)))


def _ssd_kernel(x_ref, b_ref, c_ref, z_ref, dtc_ref, dtp_ref, dtbc_ref, alc_ref, dtbp_ref, alp_ref, dsk_ref,
                wn_ref, ep_ref, el_ref, o_ref, st_ref, *, L, n_chunks, hpg, p):
    @pl.when(pl.program_id(2) == 0)
    def _():
        st_ref[...] = jnp.zeros_like(st_ref)

    npairs = hpg // 2
    a_c = -jnp.exp(alc_ref[0])
    a_p = -jnp.exp(alp_ref[0])
    tri = (lax.broadcasted_iota(jnp.int32, (L, L), 1) <= lax.broadcasted_iota(jnp.int32, (L, L), 0)).astype(BF16)
    r2 = lax.broadcasted_iota(jnp.int32, (2 * L, 2 * L), 0)
    c2 = lax.broadcasted_iota(jnp.int32, (2 * L, 2 * L), 1)
    bd = ((r2 <= c2) & ((r2 < L) == (c2 < L))).astype(BF16)
    lane2 = lax.broadcasted_iota(jnp.int32, (L, 2 * L), 1)
    row2 = lax.broadcasted_iota(jnp.int32, (L, 2 * L), 0)
    tri2 = jnp.where(lane2 < L, lane2, lane2 - L) <= row2
    lane_p = lax.broadcasted_iota(jnp.int32, (L, 2 * p), 1)

    def chunk(c, carry):
        r0 = pl.multiple_of(c * L, L)
        x = x_ref[pl.ds(r0, L), :].astype(F32)
        bmat = b_ref[pl.ds(r0, L), :]
        cmat = c_ref[pl.ds(r0, L), :]
        dt_c = _softplus(dtc_ref[0, pl.ds(r0, L), :] + dtbc_ref[0])
        cum_c = _dot_exact_lhs(tri, dt_c * a_c)
        p0 = pl.multiple_of(c * npairs, npairs)
        dt_p = _softplus(dtp_ref[0, pl.ds(p0, npairs), :] + dtbp_ref[0])
        cum_p = _dot_exact_rhs(dt_p * a_p, bd)
        both = _dot_exact_rhs(jnp.concatenate([cum_c, dt_c], axis=0), ep_ref[...])
        cum_e, dt_e = both[:L], both[L:]
        cum_l = cum_e if el_ref is None else _dot_exact_rhs(cum_c, el_ref[...])
        cb = lax.dot_general(cmat, bmat, (((1,), (1,)), ((), ())), preferred_element_type=F32)
        cb2 = jnp.concatenate([cb, cb], axis=1)
        x16 = x.astype(BF16)
        ys = []
        for j in range(npairs):
            seg = cum_l[:, j * 2 * L:(j + 1) * 2 * L] - cum_p[j:j + 1, :]
            w = jnp.exp(jnp.where(tri2, seg, NEG_BIG)) * cb2 * dt_p[j:j + 1, :]
            xp = x16[:, j * 2 * p:(j + 1) * 2 * p]
            rhs = jnp.concatenate([jnp.where(lane_p < p, xp, 0), jnp.where(lane_p >= p, xp, 0)], axis=0)
            ys.append(jnp.dot(w.astype(BF16), rhs, preferred_element_type=F32))
        y = jnp.concatenate(ys, axis=1)
        st = st_ref[...]
        y = y + jnp.dot(cmat, st.astype(BF16), preferred_element_type=F32) * jnp.exp(cum_e)
        cum_last = cum_e[L - 1:L, :]
        xw = (x * (jnp.exp(cum_last - cum_e) * dt_e)).astype(BF16)
        st_ref[...] = jnp.exp(cum_last) * st + jnp.dot(bmat.astype(F32).T.astype(BF16), xw,
                                                       preferred_element_type=F32)
        y = y + x * dsk_ref[0]
        zf = z_ref[pl.ds(r0, L), :].astype(F32)
        y = y * (zf * jax.nn.sigmoid(zf))
        y = y * lax.rsqrt(jnp.mean(y * y, axis=1, keepdims=True) + NORM_EPS) * wn_ref[0]
        o_ref[pl.ds(r0, L), :] = y.astype(o_ref.dtype)
        return carry

    lax.fori_loop(0, n_chunks, chunk, 0, unroll=4)


def _kernel_no_el(*refs, **kw):
    *ins, o_ref, st_ref = refs
    _ssd_kernel(*ins, None, o_ref, st_ref, **kw)


def ssd_core(proj, xbc, dt_raw, bsz, dt_bias, a_log, d_skip, w_norm, t_rows=256):
    m = proj.shape[0]
    s = m // bsz
    nh = dt_bias.shape[0]
    p, n, g, L = SSM_HEADDIM, SSM_STATE, SSM_GROUPS, SSM_CHUNK
    di = nh * p
    hpg = nh // g
    gw = hpg * p
    t_rows = min(t_rows, s)
    assert s % t_rows == 0 and t_rows % L == 0 and hpg % 2 == 0 and 2 * L == LANES
    nt = s // t_rows
    n_chunks = t_rows // L
    npairs = hpg // 2
    nc_all = m // L
    dt_c = dt_raw.reshape(m, g, hpg).transpose(1, 0, 2)
    dt_p = (dt_raw.reshape(nc_all, L, g, npairs, 2).transpose(2, 0, 3, 4, 1)
            .reshape(g, nc_all * npairs, 2 * L))
    col = lambda a: a.astype(F32).reshape(g, 1, hpg)
    pair = lambda a: jnp.broadcast_to(a.astype(F32).reshape(g, npairs, 2, 1), (g, npairs, 2, L)).reshape(g, npairs, 2 * L)
    expand = lambda rep: jnp.repeat(jnp.eye(hpg, dtype=BF16), rep, axis=1)
    d_e = jnp.repeat(d_skip.astype(F32), p).reshape(g, 1, gw)
    same = p == L
    row = lambda b, gg, t: b * nt + t
    gspec = lambda shape: pl.BlockSpec((1,) + shape, lambda b, gg, t: (gg, 0, 0))
    in_specs = [pl.BlockSpec((t_rows, gw), lambda b, gg, t: (row(b, gg, t), gg)),
                pl.BlockSpec((t_rows, n), lambda b, gg, t: (row(b, gg, t), di // n + gg)),
                pl.BlockSpec((t_rows, n), lambda b, gg, t: (row(b, gg, t), di // n + g + gg)),
                pl.BlockSpec((t_rows, gw), lambda b, gg, t: (row(b, gg, t), gg)),
                pl.BlockSpec((1, t_rows, hpg), lambda b, gg, t: (gg, row(b, gg, t), 0)),
                pl.BlockSpec((1, n_chunks * npairs, 2 * L), lambda b, gg, t: (gg, row(b, gg, t), 0)),
                gspec((1, hpg)), gspec((1, hpg)), gspec((npairs, 2 * L)), gspec((npairs, 2 * L)),
                gspec((1, gw)), gspec((1, gw)),
                pl.BlockSpec((hpg, gw), lambda b, gg, t: (0, 0))]
    args = [xbc, xbc, xbc, proj, dt_c, dt_p, col(dt_bias), col(a_log), pair(dt_bias), pair(a_log), d_e,
            w_norm.astype(F32).reshape(g, 1, gw), expand(p)]
    if not same:
        in_specs.append(pl.BlockSpec((hpg, hpg * L), lambda b, gg, t: (0, 0)))
        args.append(expand(L))
    kern = functools.partial(_ssd_kernel if not same else _kernel_no_el, L=L, n_chunks=n_chunks, hpg=hpg, p=p)
    return pl.pallas_call(
        kern,
        grid=(bsz, g, nt),
        in_specs=in_specs,
        out_specs=pl.BlockSpec((t_rows, gw), lambda b, gg, t: (row(b, gg, t), gg)),
        out_shape=jax.ShapeDtypeStruct((m, di), BF16),
        scratch_shapes=[pltpu.VMEM((n, gw), F32)],
        compiler_params=_cparams("parallel", "parallel", "arbitrary"),
        name="ssd_core",
    )(*args)


def mamba2_mixer(hn, bsz, w_in, conv_w, conv_b, dt_bias, a_log, d_skip, w_norm, w_out, x_res):
    nh = dt_bias.shape[0]
    di = nh * SSM_HEADDIM
    n_main = w_in.shape[1] - nh
    proj = matmul(hn, w_in[:, :n_main].astype(BF16), out_dtype=BF16)
    dt_raw = matmul(hn, w_in[:, n_main:].astype(BF16), tn=LANES)
    xbc = dwconv_silu(proj, di, conv_w, conv_b, bsz)
    y = ssd_core(proj, xbc, dt_raw, bsz, dt_bias, a_log, d_skip, w_norm)
    return matmul(y, w_out.astype(BF16), residual=x_res, tm=512, tk=w_out.shape[0])


def _hgrn2_kernel(q_ref, f_ref, i_ref, g_ref, lb_ref, wn_ref, o_ref, st_ref, *, L, sub, n_chunks, layer):
    @pl.when(pl.program_id(2) == 0)
    def _():
        st_ref[...] = jnp.zeros_like(st_ref)

    dk = q_ref.shape[1]
    lbp = lb_ref[...]
    e = jnp.exp(lbp - jnp.max(lbp, axis=0, keepdims=True))
    sm = e / jnp.sum(e, axis=0, keepdims=True)
    lb = jnp.sum(sm[1:layer + 1], axis=0, keepdims=True) if layer > 0 else jnp.zeros((1, dk), F32)
    log_lb = jnp.log(lb)
    log_1mlb = jnp.log1p(-lb)
    tri = (lax.broadcasted_iota(jnp.int32, (L, L), 1) <= lax.broadcasted_iota(jnp.int32, (L, L), 0)).astype(BF16)
    sub_row = lax.broadcasted_iota(jnp.int32, (sub, dk), 0)
    lane_s = lax.broadcasted_iota(jnp.int32, (sub, sub), 1)
    nsub = L // sub

    def chunk(c, carry):
        r0 = pl.multiple_of(c * L, L)
        qf = q_ref[pl.ds(r0, L), :].astype(F32)
        qa = qf * jax.nn.sigmoid(qf)
        fx = f_ref[pl.ds(r0, L), :].astype(F32)
        key = (1.0 - lb) * jax.nn.sigmoid(-fx)
        log_f = jnp.logaddexp(log_lb, log_1mlb + jax.nn.log_sigmoid(fx))
        b = _dot_exact_lhs(tri, log_f)
        v = i_ref[pl.ds(r0, L), :]
        strips = []
        for blk in range(nsub):
            t0 = blk * sub
            bq = b[t0:t0 + sub]
            qb = qa[t0:t0 + sub]
            cols = []
            if blk > 0:
                ref_row = b[t0 - 1:t0]
                qd = (qb * jnp.exp(bq - ref_row)).astype(BF16)
                kd = (key[:t0] * jnp.exp(ref_row - b[:t0])).astype(BF16)
                cols.append(lax.dot_general(qd, kd, (((1,), (1,)), ((), ())), preferred_element_type=F32))
            diag = jnp.zeros((sub, sub), F32)
            for sl in range(sub):
                srow = t0 + sl
                dec = jnp.exp(jnp.where(sub_row >= sl, bq - b[srow:srow + 1], NEG_BIG))
                col = jnp.sum(qb * dec * key[srow:srow + 1], axis=1, keepdims=True)
                diag = jnp.where(lane_s == sl, col, diag)
            cols.append(diag)
            if t0 + sub < L:
                cols.append(jnp.zeros((sub, L - t0 - sub), F32))
            strips.append(jnp.concatenate(cols, axis=1))
        attn = jnp.concatenate(strips, axis=0)
        st = st_ref[...]
        o = (jnp.dot(attn.astype(BF16), v, preferred_element_type=F32)
             + lax.dot_general((qa * jnp.exp(b)).astype(BF16), st.astype(BF16), (((1,), (1,)), ((), ())),
                               preferred_element_type=F32))
        b_last = b[L - 1:L]
        kd = (key * jnp.exp(b_last - b)).astype(BF16)
        st_ref[...] = jnp.exp(b_last) * st + jnp.dot(v.astype(F32).T.astype(BF16), kd, preferred_element_type=F32)
        o = o * lax.rsqrt(jnp.mean(o * o, axis=1, keepdims=True) + NORM_EPS) * wn_ref[...]
        gf = g_ref[pl.ds(r0, L), :].astype(F32)
        o_ref[pl.ds(r0, L), :] = (o * (gf * jax.nn.sigmoid(gf))).astype(o_ref.dtype)
        return carry

    lax.fori_loop(0, n_chunks, chunk, 0, unroll=8)


def hgrn2_core(proj, bsz, lb_params, layer, w_gnorm, t_rows=512):
    m, d4 = proj.shape
    d = d4 // 4
    s = m // bsz
    dk, L = HGRN_DK, HGRN_CHUNK
    n_heads = d // dk
    assert d // n_heads == dk, "kernel assumes DV == DK"
    t_rows = min(t_rows, s)
    assert s % t_rows == 0 and t_rows % L == 0
    nt = s // t_rows
    depth = lb_params.shape[0]
    sec = lambda k: pl.BlockSpec((t_rows, dk), lambda b, h, t: (b * nt + t, k * n_heads + h))
    return pl.pallas_call(
        functools.partial(_hgrn2_kernel, L=L, sub=SUBLANES, n_chunks=t_rows // L, layer=layer),
        grid=(bsz, n_heads, nt),
        in_specs=[sec(0), sec(1), sec(2), sec(3),
                  pl.BlockSpec((depth, dk), lambda b, h, t: (0, h)),
                  pl.BlockSpec((1, dk), lambda b, h, t: (0, 0))],
        out_specs=pl.BlockSpec((t_rows, dk), lambda b, h, t: (b * nt + t, h)),
        out_shape=jax.ShapeDtypeStruct((m, d), BF16),
        scratch_shapes=[pltpu.VMEM((dk, dk), F32)],
        compiler_params=_cparams("parallel", "parallel", "arbitrary"),
        name="hgrn2_core",
    )(proj, proj, proj, proj, lb_params.astype(F32), w_gnorm.reshape(1, dk).astype(F32))


def hgrn2_mixer(hn, bsz, w_in, lb_params, layer, w_gnorm, w_out, x_res):
    proj = matmul(hn, w_in.astype(BF16), out_dtype=BF16)
    o = hgrn2_core(proj, bsz, lb_params, layer, w_gnorm)
    return matmul(o, w_out.astype(BF16), residual=x_res)


def conv_ffn(x2d, bsz, g_norm, w_up, conv_w, conv_b, w_down):
    hn = rmsnorm(x2d, g_norm)
    a = ffn_up_conv_act(hn, w_up.astype(BF16), conv_w, conv_b, bsz)
    return matmul(a, w_down.astype(BF16), residual=x2d, tm=512, tk=w_down.shape[0])


def kernel(x, norm_mix, norm_ffn, norm_final, mlstm_w_in, mlstm_b_gates, mlstm_w_norm, mlstm_w_out, diff_w_in, diff_lam_q1, diff_lam_k1, diff_lam_q2, diff_lam_k2, diff_w_subln, diff_w_out, ssm_w_in, ssm_conv_w, ssm_conv_b, ssm_dt_bias, ssm_a_log, ssm_d, ssm_w_norm, ssm_w_out, hgrn_w_in, hgrn_lb, hgrn_w_gnorm, hgrn_w_out, ffn_w_up, ffn_conv_w, ffn_conv_b, ffn_w_down):
    bsz, s, d = x.shape
    depth = norm_mix.shape[0]
    x2d = x.reshape(bsz * s, d)
    for layer in range(depth):
        kind, j = layer % N_MIXERS, layer // N_MIXERS
        hn = rmsnorm(x2d, norm_mix[layer])
        if kind == 0:
            x2d = mlstm_mixer(hn, bsz, mlstm_w_in[j], mlstm_b_gates[j], mlstm_w_norm[j], mlstm_w_out[j], x2d)
        elif kind == 1:
            lambda_init = 0.8 - 0.6 * math.exp(-0.3 * layer)
            x2d = diff_attn_mixer(hn, bsz, diff_w_in[j], diff_lam_q1[j], diff_lam_k1[j], diff_lam_q2[j],
                                  diff_lam_k2[j], diff_w_subln[j], diff_w_out[j], lambda_init, x2d)
        elif kind == 2:
            x2d = mamba2_mixer(hn, bsz, ssm_w_in[j], ssm_conv_w[j], ssm_conv_b[j], ssm_dt_bias[j],
                               ssm_a_log[j], ssm_d[j], ssm_w_norm[j], ssm_w_out[j], x2d)
        else:
            x2d = hgrn2_mixer(hn, bsz, hgrn_w_in[j], hgrn_lb, layer, hgrn_w_gnorm[j], hgrn_w_out[j], x2d)
        x2d = conv_ffn(x2d, bsz, norm_ffn[layer], ffn_w_up[layer], ffn_conv_w[layer], ffn_conv_b[layer],
                       ffn_w_down[layer])
    return rmsnorm(x2d, norm_final, out_dtype=x.dtype).reshape(bsz, s, d)
```

```python
import functools
import math

import jax
import jax.numpy as jnp
from jax import lax
from jax.experimental import pallas as pl
from jax.experimental.pallas import tpu as pltpu

NORM_EPS = 1e-6
N_MIXERS = 4
DIFF_DH = 128
SSM_HEADDIM = 64
SSM_STATE = 128
SSM_GROUPS = 8
SSM_CHUNK = 64
HGRN_DK = 128
HGRN_CHUNK = 64
LOG2E = 1.4426950408889634
NEG_BIG = -1e30

V7X_VMEM_BYTES = 64 * 1024 * 1024
VMEM_LIMIT_BYTES = 56 * 1024 * 1024
LANES = 128
SUBLANES = 8

F32 = jnp.float32
BF16 = jnp.bfloat16


def _cparams(*sem):
    return pltpu.CompilerParams(dimension_semantics=sem, vmem_limit_bytes=VMEM_LIMIT_BYTES)


def _lane_tile(x, n):
    return x if n == 1 else jnp.concatenate([x] * n, axis=1)


def _mm_kernel(x_ref, w_ref, *rest, has_res, nk):
    if has_res:
        r_ref, o_ref = rest
    else:
        (o_ref,) = rest
    part = jnp.dot(x_ref[...], w_ref[...], preferred_element_type=F32)
    if nk == 1:
        if has_res:
            part = part + r_ref[...]
        o_ref[...] = part.astype(o_ref.dtype)
    else:
        k = pl.program_id(2)

        @pl.when(k == 0)
        def _():
            if has_res:
                o_ref[...] = r_ref[...] + part
            else:
                o_ref[...] = part

        @pl.when(k > 0)
        def _():
            o_ref[...] += part


def _pick_tile(n, target, quantum):
    best = None
    t = quantum
    while t <= min(n, target):
        if n % t == 0:
            best = t
        t += quantum
    assert best is not None, (n, target, quantum)
    return best


def matmul(x, w, residual=None, out_dtype=F32, tm=1024, tn=512, tk=4096, col0=0, n_out=None):
    m, k = x.shape
    k2 = w.shape[0]
    n = w.shape[1] - col0 if n_out is None else n_out
    assert k == k2
    tm = _pick_tile(m, tm, SUBLANES)
    tn = _pick_tile(math.gcd(n, col0) if col0 else n, tn, LANES)
    tk = k if k <= tk else _pick_tile(k, max(tk, 5504), LANES)
    nk = k // tk
    if nk > 1:
        assert out_dtype == F32
    has_res = residual is not None
    jb0 = col0 // tn
    in_specs = [pl.BlockSpec((tm, tk), lambda i, j, kk: (i, kk)),
                pl.BlockSpec((tk, tn), lambda i, j, kk: (kk, jb0 + j))]
    args = [x, w]
    if has_res:
        in_specs.append(pl.BlockSpec((tm, tn), lambda i, j, kk: (i, j)))
        args.append(residual)
    return pl.pallas_call(
        functools.partial(_mm_kernel, has_res=has_res, nk=nk),
        grid=(m // tm, n // tn, nk),
        in_specs=in_specs,
        out_specs=pl.BlockSpec((tm, tn), lambda i, j, kk: (i, j)),
        out_shape=jax.ShapeDtypeStruct((m, n), out_dtype),
        compiler_params=_cparams("parallel", "parallel", "arbitrary"),
        name="matmul",
    )(*args)


def _rmsnorm_kernel(x_ref, g_ref, o_ref):
    x = x_ref[...].astype(F32)
    ms = jnp.mean(x * x, axis=-1, keepdims=True)
    o_ref[...] = (x * lax.rsqrt(ms + NORM_EPS) * g_ref[...]).astype(o_ref.dtype)


def rmsnorm(x, g, out_dtype=BF16, tm=512):
    m, d = x.shape
    tm = _pick_tile(m, tm, SUBLANES)
    return pl.pallas_call(
        _rmsnorm_kernel,
        grid=(m // tm,),
        in_specs=[pl.BlockSpec((tm, d), lambda i: (i, 0)),
                  pl.BlockSpec((1, d), lambda i: (0, 0))],
        out_specs=pl.BlockSpec((tm, d), lambda i: (i, 0)),
        out_shape=jax.ShapeDtypeStruct((m, d), out_dtype),
        compiler_params=_cparams("parallel"),
        name="rmsnorm",
    )(x, g.reshape(1, d).astype(F32))


def _shift_rows(cur, prev_tail, shift):
    rolled = pltpu.roll(cur, shift, axis=0)
    tail = pltpu.roll(prev_tail, shift, axis=0)
    row = lax.broadcasted_iota(jnp.int32, (SUBLANES, cur.shape[1]), 0)
    head = jnp.where(row < shift, tail, rolled[:SUBLANES])
    return jnp.concatenate([head, rolled[SUBLANES:]], axis=0)


def _ffn_up_kernel(x_ref, wg_ref, wv_ref, cg_ref, cv_ref, bg_ref, bv_ref, o_ref,
                   ug_ref, uv_ref, tg_ref, tv_ref, *, tiles_per_seq, ncb):
    i = pl.program_id(0)
    j = pl.program_id(1)

    @pl.when((i == 0) & (j == 0))
    def _():
        for r in (ug_ref, uv_ref, tg_ref, tv_ref):
            r[...] = jnp.zeros_like(r)

    x = x_ref[...]
    ug_new = jnp.dot(x, wg_ref[...], preferred_element_type=F32)
    uv_new = jnp.dot(x, wv_ref[...], preferred_element_type=F32)
    jm = jnp.maximum(j - 1, 0)
    seq_start = i % tiles_per_seq == 0

    def conv(u, t_ref, w_ref, b_ref):
        old = t_ref[jm]
        tail = jnp.where(seq_start, 0.0, old)
        y = (w_ref[2:3, :] * u + w_ref[1:2, :] * _shift_rows(u, tail, 1)
             + w_ref[0:1, :] * _shift_rows(u, tail, 2) + b_ref[...])
        t_ref[jm] = jnp.where(j > 0, u[-SUBLANES:], old)
        return y

    g = conv(ug_ref[...], tg_ref, cg_ref, bg_ref)
    v = conv(uv_ref[...], tv_ref, cv_ref, bv_ref)
    o_ref[...] = (g * jax.nn.sigmoid(g) * v).astype(o_ref.dtype)
    ug_ref[...] = ug_new
    uv_ref[...] = uv_new


def ffn_up_conv_act(hn, w_up, conv_w, conv_b, bsz, tm=1024, tn=256):
    m, k = hn.shape
    f = w_up.shape[1] // 2
    s = m // bsz
    tm = _pick_tile(s, tm, SUBLANES)
    tn = _pick_tile(f, tn, LANES)
    ncb = f // tn
    width = conv_w.shape[0]
    assert width == 3
    cw = conv_w.astype(F32)
    cb = conv_b.reshape(1, 2 * f).astype(F32)
    cur = lambda i, j: (0, jnp.minimum(j, ncb - 1))
    cur_v = lambda i, j: (0, jnp.minimum(j, ncb - 1) + ncb)
    prev = lambda i, j: (0, jnp.maximum(j - 1, 0))
    prev_v = lambda i, j: (0, jnp.maximum(j - 1, 0) + ncb)
    return pl.pallas_call(
        functools.partial(_ffn_up_kernel, tiles_per_seq=s // tm, ncb=ncb),
        grid=(m // tm, ncb + 1),
        in_specs=[pl.BlockSpec((tm, k), lambda i, j: (i, 0)),
                  pl.BlockSpec((k, tn), cur), pl.BlockSpec((k, tn), cur_v),
                  pl.BlockSpec((width, tn), prev), pl.BlockSpec((width, tn), prev_v),
                  pl.BlockSpec((1, tn), prev), pl.BlockSpec((1, tn), prev_v)],
        out_specs=pl.BlockSpec((tm, tn), lambda i, j: (i, jnp.maximum(j - 1, 0))),
        out_shape=jax.ShapeDtypeStruct((m, f), BF16),
        scratch_shapes=[pltpu.VMEM((tm, tn), F32), pltpu.VMEM((tm, tn), F32),
                        pltpu.VMEM((ncb, SUBLANES, tn), F32), pltpu.VMEM((ncb, SUBLANES, tn), F32)],
        compiler_params=_cparams("arbitrary", "arbitrary"),
        name="ffn_up_conv_act",
    )(hn, w_up, w_up, cw, cw, cb, cb)


def _lane_cumsum(x):
    lane = lax.broadcasted_iota(jnp.int32, x.shape, 1)
    shift = 1
    while shift < x.shape[1]:
        x = x + jnp.where(lane >= shift, pltpu.roll(x, shift, axis=1), 0.0)
        shift *= 2
    return x


def _mlstm_kernel(q_ref, k_ref, v_ref, og_ref, gi_ref, gf_ref, bi_ref, bf_ref, wn_ref, out_ref,
                  c_ref, n_ref, m_ref, *, n_chunks):
    L = LANES
    dk = q_ref.shape[1]
    dv = v_ref.shape[1]

    @pl.when(pl.program_id(2) == 0)
    def _():
        c_ref[...] = jnp.zeros_like(c_ref)
        n_ref[...] = jnp.zeros_like(n_ref)
        m_ref[...] = jnp.zeros_like(m_ref)

    row_i = lax.broadcasted_iota(jnp.int32, (L, L), 0)
    col_i = lax.broadcasted_iota(jnp.int32, (L, L), 1)
    causal = col_i <= row_i
    bias_i = bi_ref[0]
    bias_f = bf_ref[0]

    for c in range(n_chunks):
        rows = slice(c * L, (c + 1) * L)
        q = (q_ref[rows, :].astype(F32) * (dk ** -0.5)).astype(BF16)
        k = k_ref[rows, :]
        v = v_ref[rows, :]
        i_row = gi_ref[0, :, rows] + bias_i
        f_raw = gf_ref[0, :, rows] + bias_f
        lf_row = jnp.minimum(f_raw, 0.0) - jnp.log1p(jnp.exp(-jnp.abs(f_raw)))
        b_row = _lane_cumsum(lf_row)
        b_last = jnp.min(b_row, axis=1, keepdims=True)
        bm = jnp.broadcast_to(b_row, (L, L))
        im = jnp.broadcast_to(i_row, (L, L))
        bt = bm.T
        it = im.T
        m_prev = m_ref[0:1, :]
        d = jnp.where(causal, bt - bm + im, NEG_BIG)
        a = bt + m_prev
        m_t = jnp.maximum(a, jnp.max(d, axis=1, keepdims=True))
        w_intra = jnp.exp(d - m_t)
        w_inter = jnp.exp(a - m_t)
        sc = lax.dot_general(q, k, (((1,), (1,)), ((), ())), preferred_element_type=F32) * w_intra
        num = (jnp.dot(sc.astype(BF16), v, preferred_element_type=F32)
               + _lane_tile(w_inter, dv // LANES)
               * jnp.dot(q, c_ref[...].astype(BF16), preferred_element_type=F32))
        qn = jnp.sum(q.astype(F32) * n_ref[0:1, :], axis=1, keepdims=True)
        den = jnp.sum(sc, axis=1, keepdims=True) + w_inter * qn
        den = jnp.maximum(jnp.abs(den), jnp.exp(-m_t))
        hc = num / _lane_tile(den, dv // LANES)
        hc = hc * lax.rsqrt(jnp.mean(hc * hc, axis=1, keepdims=True) + NORM_EPS) * wn_ref[...]
        out_ref[rows, :] = (jax.nn.sigmoid(og_ref[rows, :].astype(F32)) * hc).astype(out_ref.dtype)

        g = b_last - bt + it
        m_new = jnp.maximum(b_last + m_prev, jnp.max(g, axis=0, keepdims=True))
        ws = jnp.exp(g - m_new)
        decay = jnp.exp(b_last + m_prev - m_new)
        kw = k.astype(F32) * _lane_tile(ws, dk // LANES)
        c_ref[...] = (_lane_tile(decay, dv // LANES) * c_ref[...]
                      + jnp.dot(kw.T.astype(BF16), v, preferred_element_type=F32))
        n_ref[...] = (_lane_tile(decay, dk // LANES) * n_ref[...]
                      + jnp.sum(kw, axis=0, keepdims=True))
        m_ref[...] = jnp.broadcast_to(m_new, m_ref.shape)


def mlstm_core(proj, gates_t, bsz, b_gates, w_norm, t_rows=512):
    m = proj.shape[0]
    s = m // bsz
    n_heads = b_gates.shape[0] // 2
    dv = w_norm.shape[0] // n_heads
    dk = dv // 2
    t_rows = min(t_rows, s)
    assert s % t_rows == 0 and t_rows % LANES == 0 and dk % LANES == 0
    nt = s // t_rows
    bg = jnp.broadcast_to(b_gates.astype(F32)[:, None, None], (2 * n_heads, 1, LANES))
    row = lambda b, h, t: b * nt + t
    return pl.pallas_call(
        functools.partial(_mlstm_kernel, n_chunks=t_rows // LANES),
        grid=(bsz, n_heads, nt),
        in_specs=[pl.BlockSpec((t_rows, dk), lambda b, h, t: (row(b, h, t), h)),
                  pl.BlockSpec((t_rows, dk), lambda b, h, t: (row(b, h, t), n_heads + h)),
                  pl.BlockSpec((t_rows, dv), lambda b, h, t: (row(b, h, t), n_heads + h)),
                  pl.BlockSpec((t_rows, dv), lambda b, h, t: (row(b, h, t), 2 * n_heads + h)),
                  pl.BlockSpec((1, 1, t_rows), lambda b, h, t: (h, 0, row(b, h, t))),
                  pl.BlockSpec((1, 1, t_rows), lambda b, h, t: (n_heads + h, 0, row(b, h, t))),
                  pl.BlockSpec((1, 1, LANES), lambda b, h, t: (h, 0, 0)),
                  pl.BlockSpec((1, 1, LANES), lambda b, h, t: (n_heads + h, 0, 0)),
                  pl.BlockSpec((1, dv), lambda b, h, t: (0, h))],
        out_specs=pl.BlockSpec((t_rows, dv), lambda b, h, t: (row(b, h, t), h)),
        out_shape=jax.ShapeDtypeStruct((m, n_heads * dv), BF16),
        scratch_shapes=[pltpu.VMEM((dk, dv), F32), pltpu.VMEM((SUBLANES, dk), F32),
                        pltpu.VMEM((SUBLANES, LANES), F32)],
        compiler_params=_cparams("parallel", "parallel", "arbitrary"),
        name="mlstm_core",
    )(proj, proj, proj, proj, gates_t, gates_t, bg, bg, w_norm.reshape(1, n_heads * dv).astype(F32))


def mlstm_mixer(hn, bsz, w_in, b_gates, w_norm, w_out, x_res):
    n_heads = b_gates.shape[0] // 2
    n_main = w_in.shape[1] - 2 * n_heads
    w_g = jnp.pad(w_in[:, n_main:], ((0, 0), (0, LANES - 2 * n_heads))).astype(BF16)
    proj = matmul(hn, w_in.astype(BF16), out_dtype=BF16, tn=1024, n_out=n_main)
    gates_t = matmul(hn, w_g, tn=LANES)[:, :2 * n_heads].T[:, None, :]
    hh = mlstm_core(proj, gates_t, bsz, b_gates, w_norm)
    return matmul(hh, w_out.astype(BF16), residual=x_res)


def _diff_attn_kernel(q_ref, k_ref, v_ref, sl_ref, lq1_ref, lk1_ref, lq2_ref, lk2_ref, g_ref, o_ref,
                      m_ref, l_ref, acc_ref, *, tq, tk, lambda_init):
    qi = pl.program_id(2)
    dh = DIFF_DH
    q = (q_ref[...].astype(F32) * (dh ** -0.5 * LOG2E)).astype(BF16)
    slope_row = _lane_tile(sl_ref[0], tk // LANES)
    col = lax.broadcasted_iota(jnp.int32, (1, tk), 1).astype(F32)
    m_ref[...] = jnp.full(m_ref.shape, NEG_BIG, F32)
    l_ref[...] = jnp.zeros(l_ref.shape, F32)
    acc_ref[...] = jnp.zeros(acc_ref.shape, F32)

    def step(kk, masked):
        start = pl.multiple_of(kk * tk, tk)
        cb = (col + (kk * tk - qi * tq).astype(F32)) * slope_row
        kblk = k_ref[pl.ds(start, tk), :]
        vblk = v_ref[pl.ds(start, tk), :]
        for c in range(2):
            s = lax.dot_general(q[:, c * dh:(c + 1) * dh], kblk[:, c * dh:(c + 1) * dh],
                                (((1,), (1,)), ((), ())), preferred_element_type=F32)
            s = s + cb
            if masked:
                row_i = lax.broadcasted_iota(jnp.int32, (tq, tk), 0)
                col_i = lax.broadcasted_iota(jnp.int32, (tq, tk), 1)
                s = jnp.where(col_i <= row_i, s, NEG_BIG)
            m_prev = m_ref[c]
            m_new = jnp.maximum(m_prev, jnp.max(s, axis=1, keepdims=True))
            alpha = jnp.exp2(m_prev - m_new)
            p = jnp.exp2(s - _lane_tile(m_new, tk // LANES))
            l_ref[c] = alpha * l_ref[c] + jnp.sum(p, axis=1, keepdims=True)
            m_ref[c] = m_new
            acc_ref[c] = (acc_ref[c] * _lane_tile(alpha, 2 * dh // LANES)
                          + jnp.dot(p.astype(BF16), vblk, preferred_element_type=F32))

    def body(kk, carry):
        step(kk, False)
        return carry

    lax.fori_loop(0, qi, body, 0)
    step(qi, True)

    lam = (jnp.exp(jnp.sum(lq1_ref[...] * lk1_ref[...])) - jnp.exp(jnp.sum(lq2_ref[...] * lk2_ref[...]))
           + lambda_init)
    rep = 2 * dh // LANES
    o = (acc_ref[0] / _lane_tile(l_ref[0], rep)
         - lam * (acc_ref[1] / _lane_tile(l_ref[1], rep)))
    o = o * lax.rsqrt(jnp.mean(o * o, axis=-1, keepdims=True) + NORM_EPS) * g_ref[...]
    o_ref[...] = (o * (1.0 - lambda_init)).astype(o_ref.dtype)


def diff_attn_core(proj, bsz, lam_q1, lam_k1, lam_q2, lam_k2, w_subln, lambda_init, tq=1024):
    m, d3 = proj.shape
    d = d3 // 3
    s = m // bsz
    hw = 2 * DIFF_DH
    n_heads = d // hw
    tq = min(tq, s)
    assert s % tq == 0
    nq = s // tq
    slopes = jnp.exp2(-8.0 * jnp.arange(1, n_heads + 1, dtype=F32) / n_heads) * LOG2E
    slopes = jnp.broadcast_to(slopes[:, None, None], (n_heads, 1, LANES))
    vec = lambda a: a.reshape(1, -1).astype(F32)
    small = pl.BlockSpec((1, DIFF_DH), lambda b, h, i: (0, 0))
    return pl.pallas_call(
        functools.partial(_diff_attn_kernel, tq=tq, tk=tq, lambda_init=lambda_init),
        grid=(bsz, n_heads, nq),
        in_specs=[pl.BlockSpec((tq, hw), lambda b, h, i: (b * nq + i, h)),
                  pl.BlockSpec((s, hw), lambda b, h, i: (b, n_heads + h)),
                  pl.BlockSpec((s, hw), lambda b, h, i: (b, 2 * n_heads + h)),
                  pl.BlockSpec((1, 1, LANES), lambda b, h, i: (h, 0, 0)),
                  small, small, small, small,
                  pl.BlockSpec((1, hw), lambda b, h, i: (0, 0))],
        out_specs=pl.BlockSpec((tq, hw), lambda b, h, i: (b * nq + i, h)),
        out_shape=jax.ShapeDtypeStruct((m, d), BF16),
        scratch_shapes=[pltpu.VMEM((2, tq, LANES), F32), pltpu.VMEM((2, tq, LANES), F32),
                        pltpu.VMEM((2, tq, hw), F32)],
        compiler_params=_cparams("parallel", "parallel", "arbitrary"),
        name="diff_attn",
    )(proj, proj, proj, slopes, vec(lam_q1), vec(lam_k1), vec(lam_q2), vec(lam_k2), vec(w_subln))


def diff_attn_mixer(hn, bsz, w_in, lam_q1, lam_k1, lam_q2, lam_k2, w_subln, w_out, lambda_init, x_res):
    proj = matmul(hn, w_in.astype(BF16), out_dtype=BF16, tn=1024)
    o = diff_attn_core(proj, bsz, lam_q1, lam_k1, lam_q2, lam_k2, w_subln, lambda_init)
    return matmul(o, w_out.astype(BF16), residual=x_res)


def _dwconv_silu_kernel(u_ref, w_ref, b_ref, o_ref, tail_ref, *, width):
    @pl.when(pl.program_id(2) == 0)
    def _():
        tail_ref[...] = jnp.zeros_like(tail_ref)

    u = u_ref[...].astype(F32)
    tail = tail_ref[...]
    y = w_ref[width - 1:width, :] * u + b_ref[...]
    for j in range(width - 1):
        y = y + w_ref[j:j + 1, :] * _shift_rows(u, tail, width - 1 - j)
    tail_ref[...] = u[-SUBLANES:]
    o_ref[...] = (y * jax.nn.sigmoid(y)).astype(o_ref.dtype)


def dwconv_silu(u, col0, conv_w, conv_b, bsz, ts=512, tc=512):
    m = u.shape[0]
    width, ch = conv_w.shape
    s = m // bsz
    ts = _pick_tile(s, ts, SUBLANES)
    tc = _pick_tile(math.gcd(ch, col0) if col0 else ch, tc, LANES)
    nsb = s // ts
    c0 = col0 // tc
    return pl.pallas_call(
        functools.partial(_dwconv_silu_kernel, width=width),
        grid=(ch // tc, bsz, nsb),
        in_specs=[pl.BlockSpec((ts, tc), lambda c, b, t: (b * nsb + t, c0 + c)),
                  pl.BlockSpec((width, tc), lambda c, b, t: (0, c)),
                  pl.BlockSpec((1, tc), lambda c, b, t: (0, c))],
        out_specs=pl.BlockSpec((ts, tc), lambda c, b, t: (b * nsb + t, c)),
        out_shape=jax.ShapeDtypeStruct((m, ch), BF16),
        scratch_shapes=[pltpu.VMEM((SUBLANES, tc), F32)],
        compiler_params=_cparams("parallel", "parallel", "arbitrary"),
        name="dwconv_silu",
    )(u, conv_w.astype(F32), conv_b.reshape(1, ch).astype(F32))


def _split3(v):
    hi = v.astype(BF16)
    r = v - hi.astype(F32)
    mid = r.astype(BF16)
    lo = (r - mid.astype(F32)).astype(BF16)
    return hi, mid, lo


def _dot_exact_rhs(lhs_f32, rhs_bf16):
    n = lhs_f32.shape[0]
    out = jnp.dot(jnp.concatenate(_split3(lhs_f32), axis=0), rhs_bf16, preferred_element_type=F32)
    return out[:n] + out[n:2 * n] + out[2 * n:]


def _dot_exact_lhs(lhs_bf16, rhs_f32):
    hi, mid, lo = _split3(rhs_f32)
    return (jnp.dot(lhs_bf16, hi, preferred_element_type=F32) + jnp.dot(lhs_bf16, mid, preferred_element_type=F32)
            + jnp.dot(lhs_bf16, lo, preferred_element_type=F32))


def _softplus(x):
    return jnp.maximum(x, 0.0) + jnp.log1p(jnp.exp(-jnp.abs(x)))


def _ssd_kernel(x_ref, b_ref, c_ref, z_ref, dtc_ref, dtp_ref, dtbc_ref, alc_ref, dtbp_ref, alp_ref, dsk_ref,
                wn_ref, ep_ref, el_ref, o_ref, st_ref, *, L, n_chunks, hpg, p):
    @pl.when(pl.program_id(2) == 0)
    def _():
        st_ref[...] = jnp.zeros_like(st_ref)

    npairs = hpg // 2
    a_c = -jnp.exp(alc_ref[0])
    a_p = -jnp.exp(alp_ref[0])
    tri = (lax.broadcasted_iota(jnp.int32, (L, L), 1) <= lax.broadcasted_iota(jnp.int32, (L, L), 0)).astype(BF16)
    r2 = lax.broadcasted_iota(jnp.int32, (2 * L, 2 * L), 0)
    c2 = lax.broadcasted_iota(jnp.int32, (2 * L, 2 * L), 1)
    bd = ((r2 <= c2) & ((r2 < L) == (c2 < L))).astype(BF16)
    lane2 = lax.broadcasted_iota(jnp.int32, (L, 2 * L), 1)
    row2 = lax.broadcasted_iota(jnp.int32, (L, 2 * L), 0)
    tri2 = jnp.where(lane2 < L, lane2, lane2 - L) <= row2
    lane_p = lax.broadcasted_iota(jnp.int32, (L, 2 * p), 1)

    def chunk(c, carry):
        r0 = pl.multiple_of(c * L, L)
        x = x_ref[pl.ds(r0, L), :].astype(F32)
        bmat = b_ref[pl.ds(r0, L), :]
        cmat = c_ref[pl.ds(r0, L), :]
        dt_c = _softplus(dtc_ref[0, pl.ds(r0, L), :] + dtbc_ref[0])
        cum_c = _dot_exact_lhs(tri, dt_c * a_c)
        p0 = pl.multiple_of(c * npairs, npairs)
        dt_p = _softplus(dtp_ref[0, pl.ds(p0, npairs), :] + dtbp_ref[0])
        cum_p = _dot_exact_rhs(dt_p * a_p, bd)
        both = _dot_exact_rhs(jnp.concatenate([cum_c, dt_c], axis=0), ep_ref[...])
        cum_e, dt_e = both[:L], both[L:]
        cum_l = cum_e if el_ref is None else _dot_exact_rhs(cum_c, el_ref[...])
        cb = lax.dot_general(cmat, bmat, (((1,), (1,)), ((), ())), preferred_element_type=F32)
        cb2 = jnp.concatenate([cb, cb], axis=1)
        x16 = x.astype(BF16)
        ys = []
        for j in range(npairs):
            seg = cum_l[:, j * 2 * L:(j + 1) * 2 * L] - cum_p[j:j + 1, :]
            w = jnp.exp(jnp.where(tri2, seg, NEG_BIG)) * cb2 * dt_p[j:j + 1, :]
            xp = x16[:, j * 2 * p:(j + 1) * 2 * p]
            rhs = jnp.concatenate([jnp.where(lane_p < p, xp, 0), jnp.where(lane_p >= p, xp, 0)], axis=0)
            ys.append(jnp.dot(w.astype(BF16), rhs, preferred_element_type=F32))
        y = jnp.concatenate(ys, axis=1)
        st = st_ref[...]
        y = y + jnp.dot(cmat, st.astype(BF16), preferred_element_type=F32) * jnp.exp(cum_e)
        cum_last = cum_e[L - 1:L, :]
        xw = (x * (jnp.exp(cum_last - cum_e) * dt_e)).astype(BF16)
        st_ref[...] = jnp.exp(cum_last) * st + jnp.dot(bmat.astype(F32).T.astype(BF16), xw,
                                                       preferred_element_type=F32)
        y = y + x * dsk_ref[0]
        zf = z_ref[pl.ds(r0, L), :].astype(F32)
        y = y * (zf * jax.nn.sigmoid(zf))
        y = y * lax.rsqrt(jnp.mean(y * y, axis=1, keepdims=True) + NORM_EPS) * wn_ref[0]
        o_ref[pl.ds(r0, L), :] = y.astype(o_ref.dtype)
        return carry

    lax.fori_loop(0, n_chunks, chunk, 0, unroll=8)


def _kernel_no_el(*refs, **kw):
    *ins, o_ref, st_ref = refs
    _ssd_kernel(*ins, None, o_ref, st_ref, **kw)


def ssd_core(proj, xbc, dt_raw, bsz, dt_bias, a_log, d_skip, w_norm, t_rows=512):
    m = proj.shape[0]
    s = m // bsz
    nh = dt_bias.shape[0]
    p, n, g, L = SSM_HEADDIM, SSM_STATE, SSM_GROUPS, SSM_CHUNK
    di = nh * p
    hpg = nh // g
    gw = hpg * p
    t_rows = min(t_rows, s)
    assert s % t_rows == 0 and t_rows % L == 0 and hpg % 2 == 0 and 2 * L == LANES
    nt = s // t_rows
    n_chunks = t_rows // L
    npairs = hpg // 2
    nc_all = m // L
    dt_c = dt_raw.reshape(m, g, hpg).transpose(1, 0, 2)
    dt_p = (dt_raw.reshape(nc_all, L, g, npairs, 2).transpose(2, 0, 3, 4, 1)
            .reshape(g, nc_all * npairs, 2 * L))
    col = lambda a: a.astype(F32).reshape(g, 1, hpg)
    pair = lambda a: jnp.broadcast_to(a.astype(F32).reshape(g, npairs, 2, 1), (g, npairs, 2, L)).reshape(g, npairs, 2 * L)
    expand = lambda rep: jnp.repeat(jnp.eye(hpg, dtype=BF16), rep, axis=1)
    d_e = jnp.repeat(d_skip.astype(F32), p).reshape(g, 1, gw)
    same = p == L
    row = lambda b, gg, t: b * nt + t
    gspec = lambda shape: pl.BlockSpec((1,) + shape, lambda b, gg, t: (gg, 0, 0))
    in_specs = [pl.BlockSpec((t_rows, gw), lambda b, gg, t: (row(b, gg, t), gg)),
                pl.BlockSpec((t_rows, n), lambda b, gg, t: (row(b, gg, t), di // n + gg)),
                pl.BlockSpec((t_rows, n), lambda b, gg, t: (row(b, gg, t), di // n + g + gg)),
                pl.BlockSpec((t_rows, gw), lambda b, gg, t: (row(b, gg, t), gg)),
                pl.BlockSpec((1, t_rows, hpg), lambda b, gg, t: (gg, row(b, gg, t), 0)),
                pl.BlockSpec((1, n_chunks * npairs, 2 * L), lambda b, gg, t: (gg, row(b, gg, t), 0)),
                gspec((1, hpg)), gspec((1, hpg)), gspec((npairs, 2 * L)), gspec((npairs, 2 * L)),
                gspec((1, gw)), gspec((1, gw)),
                pl.BlockSpec((hpg, gw), lambda b, gg, t: (0, 0))]
    args = [xbc, xbc, xbc, proj, dt_c, dt_p, col(dt_bias), col(a_log), pair(dt_bias), pair(a_log), d_e,
            w_norm.astype(F32).reshape(g, 1, gw), expand(p)]
    if not same:
        in_specs.append(pl.BlockSpec((hpg, hpg * L), lambda b, gg, t: (0, 0)))
        args.append(expand(L))
    kern = functools.partial(_ssd_kernel if not same else _kernel_no_el, L=L, n_chunks=n_chunks, hpg=hpg, p=p)
    return pl.pallas_call(
        kern,
        grid=(bsz, g, nt),
        in_specs=in_specs,
        out_specs=pl.BlockSpec((t_rows, gw), lambda b, gg, t: (row(b, gg, t), gg)),
        out_shape=jax.ShapeDtypeStruct((m, di), BF16),
        scratch_shapes=[pltpu.VMEM((n, gw), F32)],
        compiler_params=_cparams("parallel", "parallel", "arbitrary"),
        name="ssd_core",
    )(*args)


def mamba2_mixer(hn, bsz, w_in, conv_w, conv_b, dt_bias, a_log, d_skip, w_norm, w_out, x_res):
    nh = dt_bias.shape[0]
    di = nh * SSM_HEADDIM
    n_main = w_in.shape[1] - nh
    w_bf = w_in.astype(BF16)
    proj = matmul(hn, w_bf, out_dtype=BF16, tn=1024, n_out=n_main)
    dt_raw = matmul(hn, w_bf, tn=LANES, col0=n_main, n_out=nh)
    xbc = dwconv_silu(proj, di, conv_w, conv_b, bsz)
    y = ssd_core(proj, xbc, dt_raw, bsz, dt_bias, a_log, d_skip, w_norm)
    return matmul(y, w_out.astype(BF16), residual=x_res, tm=512, tk=w_out.shape[0])


def _hgrn2_kernel(q_ref, f_ref, i_ref, g_ref, lb_ref, wn_ref, o_ref, st_ref, *, L, sub, n_chunks, layer):
    @pl.when(pl.program_id(2) == 0)
    def _():
        st_ref[...] = jnp.zeros_like(st_ref)

    dk = q_ref.shape[1]
    lbp = lb_ref[...]
    e = jnp.exp(lbp - jnp.max(lbp, axis=0, keepdims=True))
    sm = e / jnp.sum(e, axis=0, keepdims=True)
    lb = jnp.sum(sm[1:layer + 1], axis=0, keepdims=True) if layer > 0 else jnp.zeros((1, dk), F32)
    log_lb = jnp.log(lb)
    log_1mlb = jnp.log1p(-lb)
    tri = (lax.broadcasted_iota(jnp.int32, (L, L), 1) <= lax.broadcasted_iota(jnp.int32, (L, L), 0)).astype(BF16)
    sub_row = lax.broadcasted_iota(jnp.int32, (sub, dk), 0)
    lane_s = lax.broadcasted_iota(jnp.int32, (sub, sub), 1)
    nsub = L // sub

    def chunk(c, carry):
        r0 = pl.multiple_of(c * L, L)
        qf = q_ref[pl.ds(r0, L), :].astype(F32)
        qa = qf * jax.nn.sigmoid(qf)
        fx = f_ref[pl.ds(r0, L), :].astype(F32)
        key = (1.0 - lb) * jax.nn.sigmoid(-fx)
        log_f = jnp.logaddexp(log_lb, log_1mlb + jax.nn.log_sigmoid(fx))
        b = _dot_exact_lhs(tri, log_f)
        v = i_ref[pl.ds(r0, L), :]
        strips = []
        for blk in range(nsub):
            t0 = blk * sub
            bq = b[t0:t0 + sub]
            qb = qa[t0:t0 + sub]
            cols = []
            if blk > 0:
                ref_row = b[t0 - 1:t0]
                qd = (qb * jnp.exp(bq - ref_row)).astype(BF16)
                kd = (key[:t0] * jnp.exp(ref_row - b[:t0])).astype(BF16)
                cols.append(lax.dot_general(qd, kd, (((1,), (1,)), ((), ())), preferred_element_type=F32))
            diag = jnp.zeros((sub, sub), F32)
            for sl in range(sub):
                srow = t0 + sl
                dec = jnp.exp(jnp.where(sub_row >= sl, bq - b[srow:srow + 1], NEG_BIG))
                col = jnp.sum(qb * dec * key[srow:srow + 1], axis=1, keepdims=True)
                diag = jnp.where(lane_s == sl, col, diag)
            cols.append(diag)
            if t0 + sub < L:
                cols.append(jnp.zeros((sub, L - t0 - sub), F32))
            strips.append(jnp.concatenate(cols, axis=1))
        attn = jnp.concatenate(strips, axis=0)
        st = st_ref[...]
        o = (jnp.dot(attn.astype(BF16), v, preferred_element_type=F32)
             + lax.dot_general((qa * jnp.exp(b)).astype(BF16), st.astype(BF16), (((1,), (1,)), ((), ())),
                               preferred_element_type=F32))
        b_last = b[L - 1:L]
        kd = (key * jnp.exp(b_last - b)).astype(BF16)
        st_ref[...] = jnp.exp(b_last) * st + jnp.dot(v.astype(F32).T.astype(BF16), kd, preferred_element_type=F32)
        o = o * lax.rsqrt(jnp.mean(o * o, axis=1, keepdims=True) + NORM_EPS) * wn_ref[...]
        gf = g_ref[pl.ds(r0, L), :].astype(F32)
        o_ref[pl.ds(r0, L), :] = (o * (gf * jax.nn.sigmoid(gf))).astype(o_ref.dtype)
        return carry

    lax.fori_loop(0, n_chunks, chunk, 0, unroll=8)


def hgrn2_core(proj, bsz, lb_params, layer, w_gnorm, t_rows=1024):
    m, d4 = proj.shape
    d = d4 // 4
    s = m // bsz
    dk, L = HGRN_DK, HGRN_CHUNK
    n_heads = d // dk
    assert d // n_heads == dk, "kernel assumes DV == DK"
    t_rows = min(t_rows, s)
    assert s % t_rows == 0 and t_rows % L == 0
    nt = s // t_rows
    depth = lb_params.shape[0]
    sec = lambda k: pl.BlockSpec((t_rows, dk), lambda b, h, t: (b * nt + t, k * n_heads + h))
    return pl.pallas_call(
        functools.partial(_hgrn2_kernel, L=L, sub=SUBLANES, n_chunks=t_rows // L, layer=layer),
        grid=(bsz, n_heads, nt),
        in_specs=[sec(0), sec(1), sec(2), sec(3),
                  pl.BlockSpec((depth, dk), lambda b, h, t: (0, h)),
                  pl.BlockSpec((1, dk), lambda b, h, t: (0, 0))],
        out_specs=pl.BlockSpec((t_rows, dk), lambda b, h, t: (b * nt + t, h)),
        out_shape=jax.ShapeDtypeStruct((m, d), BF16),
        scratch_shapes=[pltpu.VMEM((dk, dk), F32)],
        compiler_params=_cparams("parallel", "parallel", "arbitrary"),
        name="hgrn2_core",
    )(proj, proj, proj, proj, lb_params.astype(F32), w_gnorm.reshape(1, dk).astype(F32))


def hgrn2_mixer(hn, bsz, w_in, lb_params, layer, w_gnorm, w_out, x_res):
    proj = matmul(hn, w_in.astype(BF16), out_dtype=BF16, tn=1024)
    o = hgrn2_core(proj, bsz, lb_params, layer, w_gnorm)
    return matmul(o, w_out.astype(BF16), residual=x_res)


def conv_ffn(x2d, bsz, g_norm, w_up, conv_w, conv_b, w_down):
    hn = rmsnorm(x2d, g_norm)
    a = ffn_up_conv_act(hn, w_up.astype(BF16), conv_w, conv_b, bsz)
    return matmul(a, w_down.astype(BF16), residual=x2d, tm=512, tk=w_down.shape[0])


def kernel(x, norm_mix, norm_ffn, norm_final, mlstm_w_in, mlstm_b_gates, mlstm_w_norm, mlstm_w_out, diff_w_in, diff_lam_q1, diff_lam_k1, diff_lam_q2, diff_lam_k2, diff_w_subln, diff_w_out, ssm_w_in, ssm_conv_w, ssm_conv_b, ssm_dt_bias, ssm_a_log, ssm_d, ssm_w_norm, ssm_w_out, hgrn_w_in, hgrn_lb, hgrn_w_gnorm, hgrn_w_out, ffn_w_up, ffn_conv_w, ffn_conv_b, ffn_w_down):
    bsz, s, d = x.shape
    depth = norm_mix.shape[0]
    x2d = x.reshape(bsz * s, d)
    for layer in range(depth):
        kind, j = layer % N_MIXERS, layer // N_MIXERS
        hn = rmsnorm(x2d, norm_mix[layer])
        if kind == 0:
            x2d = mlstm_mixer(hn, bsz, mlstm_w_in[j], mlstm_b_gates[j], mlstm_w_norm[j], mlstm_w_out[j], x2d)
        elif kind == 1:
            lambda_init = 0.8 - 0.6 * math.exp(-0.3 * layer)
            x2d = diff_attn_mixer(hn, bsz, diff_w_in[j], diff_lam_q1[j], diff_lam_k1[j], diff_lam_q2[j],
                                  diff_lam_k2[j], diff_w_subln[j], diff_w_out[j], lambda_init, x2d)
        elif kind == 2:
            x2d = mamba2_mixer(hn, bsz, ssm_w_in[j], ssm_conv_w[j], ssm_conv_b[j], ssm_dt_bias[j],
                               ssm_a_log[j], ssm_d[j], ssm_w_norm[j], ssm_w_out[j], x2d)
        else:
            x2d = hgrn2_mixer(hn, bsz, hgrn_w_in[j], hgrn_lb, layer, hgrn_w_gnorm[j], hgrn_w_out[j], x2d)
        x2d = conv_ffn(x2d, bsz, norm_ffn[layer], ffn_w_up[layer], ffn_conv_w[layer], ffn_conv_b[layer],
                       ffn_w_down[layer])
    return rmsnorm(x2d, norm_final, out_dtype=x.dtype).reshape(bsz, s, d)
```

```python
import functools
import math

import jax
import jax.numpy as jnp
from jax import lax
from jax.experimental import pallas as pl
from jax.experimental.pallas import tpu as pltpu

NORM_EPS = 1e-6
N_MIXERS = 4
DIFF_DH = 128
SSM_HEADDIM = 64
SSM_STATE = 128
SSM_GROUPS = 8
SSM_CHUNK = 64
HGRN_DK = 128
HGRN_CHUNK = 64
LOG2E = 1.4426950408889634
NEG_BIG = -1e30

V7X_VMEM_BYTES = 64 * 1024 * 1024
VMEM_LIMIT_BYTES = 56 * 1024 * 1024
LANES = 128
SUBLANES = 8

F32 = jnp.float32
BF16 = jnp.bfloat16


def _cparams(*sem):
    return pltpu.CompilerParams(dimension_semantics=sem, vmem_limit_bytes=VMEM_LIMIT_BYTES)


def _lane_tile(x, n):
    return x if n == 1 else jnp.concatenate([x] * n, axis=1)


def _mm_kernel(x_ref, w_ref, *rest, has_res, nk):
    if has_res:
        r_ref, o_ref = rest
    else:
        (o_ref,) = rest
    part = jnp.dot(x_ref[...], w_ref[...], preferred_element_type=F32)
    if nk == 1:
        if has_res:
            part = part + r_ref[...]
        o_ref[...] = part.astype(o_ref.dtype)
    else:
        k = pl.program_id(2)

        @pl.when(k == 0)
        def _():
            if has_res:
                o_ref[...] = r_ref[...] + part
            else:
                o_ref[...] = part

        @pl.when(k > 0)
        def _():
            o_ref[...] += part


def _pick_tile(n, target, quantum):
    best = None
    t = quantum
    while t <= min(n, target):
        if n % t == 0:
            best = t
        t += quantum
    assert best is not None, (n, target, quantum)
    return best


def matmul(x, w, residual=None, out_dtype=F32, tm=1024, tn=512, tk=4096, col0=0, n_out=None):
    m, k = x.shape
    k2 = w.shape[0]
    n = w.shape[1] - col0 if n_out is None else n_out
    assert k == k2
    tm = _pick_tile(m, tm, SUBLANES)
    tn = _pick_tile(math.gcd(n, col0) if col0 else n, tn, LANES)
    tk = k if k <= tk else _pick_tile(k, max(tk, 5504), LANES)
    nk = k // tk
    if nk > 1:
        assert out_dtype == F32
    has_res = residual is not None
    jb0 = col0 // tn
    in_specs = [pl.BlockSpec((tm, tk), lambda i, j, kk: (i, kk)),
                pl.BlockSpec((tk, tn), lambda i, j, kk: (kk, jb0 + j))]
    args = [x, w]
    if has_res:
        in_specs.append(pl.BlockSpec((tm, tn), lambda i, j, kk: (i, j)))
        args.append(residual)
    return pl.pallas_call(
        functools.partial(_mm_kernel, has_res=has_res, nk=nk),
        grid=(m // tm, n // tn, nk),
        in_specs=in_specs,
        out_specs=pl.BlockSpec((tm, tn), lambda i, j, kk: (i, j)),
        out_shape=jax.ShapeDtypeStruct((m, n), out_dtype),
        compiler_params=_cparams("parallel", "parallel", "arbitrary"),
        name="matmul",
    )(*args)


def _rmsnorm_kernel(x_ref, g_ref, o_ref):
    x = x_ref[...].astype(F32)
    ms = jnp.mean(x * x, axis=-1, keepdims=True)
    o_ref[...] = (x * lax.rsqrt(ms + NORM_EPS) * g_ref[...]).astype(o_ref.dtype)


def rmsnorm(x, g, out_dtype=BF16, tm=512):
    m, d = x.shape
    tm = _pick_tile(m, tm, SUBLANES)
    return pl.pallas_call(
        _rmsnorm_kernel,
        grid=(m // tm,),
        in_specs=[pl.BlockSpec((tm, d), lambda i: (i, 0)),
                  pl.BlockSpec((1, d), lambda i: (0, 0))],
        out_specs=pl.BlockSpec((tm, d), lambda i: (i, 0)),
        out_shape=jax.ShapeDtypeStruct((m, d), out_dtype),
        compiler_params=_cparams("parallel"),
        name="rmsnorm",
    )(x, g.reshape(1, d).astype(F32))


def _shift_rows(cur, prev_tail, shift):
    rolled = pltpu.roll(cur, shift, axis=0)
    tail = pltpu.roll(prev_tail, shift, axis=0)
    row = lax.broadcasted_iota(jnp.int32, (SUBLANES, cur.shape[1]), 0)
    head = jnp.where(row < shift, tail, rolled[:SUBLANES])
    return jnp.concatenate([head, rolled[SUBLANES:]], axis=0)


def _ffn_up_kernel(x_ref, wg_ref, wv_ref, cg_ref, cv_ref, bg_ref, bv_ref, o_ref,
                   ug_ref, uv_ref, tg_ref, tv_ref, *, tiles_per_seq, ncb):
    i = pl.program_id(0)
    j = pl.program_id(1)

    @pl.when((i == 0) & (j == 0))
    def _():
        for r in (ug_ref, uv_ref, tg_ref, tv_ref):
            r[...] = jnp.zeros_like(r)

    x = x_ref[...]
    ug_new = jnp.dot(x, wg_ref[...], preferred_element_type=F32)
    uv_new = jnp.dot(x, wv_ref[...], preferred_element_type=F32)
    jm = jnp.maximum(j - 1, 0)
    seq_start = i % tiles_per_seq == 0

    def conv(u, t_ref, w_ref, b_ref):
        old = t_ref[jm]
        tail = jnp.where(seq_start, 0.0, old)
        y = (w_ref[2:3, :] * u + w_ref[1:2, :] * _shift_rows(u, tail, 1)
             + w_ref[0:1, :] * _shift_rows(u, tail, 2) + b_ref[...])
        t_ref[jm] = jnp.where(j > 0, u[-SUBLANES:], old)
        return y

    g = conv(ug_ref[...], tg_ref, cg_ref, bg_ref)
    v = conv(uv_ref[...], tv_ref, cv_ref, bv_ref)
    o_ref[...] = (g * jax.nn.sigmoid(g) * v).astype(o_ref.dtype)
    ug_ref[...] = ug_new
    uv_ref[...] = uv_new


def ffn_up_conv_act(hn, w_up, conv_w, conv_b, bsz, tm=1024, tn=256):
    m, k = hn.shape
    f = w_up.shape[1] // 2
    s = m // bsz
    tm = _pick_tile(s, tm, SUBLANES)
    tn = _pick_tile(f, tn, LANES)
    ncb = f // tn
    width = conv_w.shape[0]
    assert width == 3
    cw = conv_w.astype(F32)
    cb = conv_b.reshape(1, 2 * f).astype(F32)
    cur = lambda i, j: (0, jnp.minimum(j, ncb - 1))
    cur_v = lambda i, j: (0, jnp.minimum(j, ncb - 1) + ncb)
    prev = lambda i, j: (0, jnp.maximum(j - 1, 0))
    prev_v = lambda i, j: (0, jnp.maximum(j - 1, 0) + ncb)
    return pl.pallas_call(
        functools.partial(_ffn_up_kernel, tiles_per_seq=s // tm, ncb=ncb),
        grid=(m // tm, ncb + 1),
        in_specs=[pl.BlockSpec((tm, k), lambda i, j: (i, 0)),
                  pl.BlockSpec((k, tn), cur), pl.BlockSpec((k, tn), cur_v),
                  pl.BlockSpec((width, tn), prev), pl.BlockSpec((width, tn), prev_v),
                  pl.BlockSpec((1, tn), prev), pl.BlockSpec((1, tn), prev_v)],
        out_specs=pl.BlockSpec((tm, tn), lambda i, j: (i, jnp.maximum(j - 1, 0))),
        out_shape=jax.ShapeDtypeStruct((m, f), BF16),
        scratch_shapes=[pltpu.VMEM((tm, tn), F32), pltpu.VMEM((tm, tn), F32),
                        pltpu.VMEM((ncb, SUBLANES, tn), F32), pltpu.VMEM((ncb, SUBLANES, tn), F32)],
        compiler_params=_cparams("arbitrary", "arbitrary"),
        name="ffn_up_conv_act",
    )(hn, w_up, w_up, cw, cw, cb, cb)


def _lane_cumsum(x):
    lane = lax.broadcasted_iota(jnp.int32, x.shape, 1)
    shift = 1
    while shift < x.shape[1]:
        x = x + jnp.where(lane >= shift, pltpu.roll(x, shift, axis=1), 0.0)
        shift *= 2
    return x


def _mlstm_kernel(q_ref, k_ref, v_ref, og_ref, gi_ref, gf_ref, bi_ref, bf_ref, wn_ref, out_ref,
                  c_ref, n_ref, m_ref, *, n_chunks):
    L = LANES
    dk = q_ref.shape[1]
    dv = v_ref.shape[1]

    @pl.when(pl.program_id(2) == 0)
    def _():
        c_ref[...] = jnp.zeros_like(c_ref)
        n_ref[...] = jnp.zeros_like(n_ref)
        m_ref[...] = jnp.zeros_like(m_ref)

    row_i = lax.broadcasted_iota(jnp.int32, (L, L), 0)
    col_i = lax.broadcasted_iota(jnp.int32, (L, L), 1)
    causal = col_i <= row_i
    bias_i = bi_ref[0]
    bias_f = bf_ref[0]

    for c in range(n_chunks):
        rows = slice(c * L, (c + 1) * L)
        q = (q_ref[rows, :].astype(F32) * (dk ** -0.5)).astype(BF16)
        k = k_ref[rows, :]
        v = v_ref[rows, :]
        i_row = gi_ref[0, :, rows] + bias_i
        f_raw = gf_ref[0, :, rows] + bias_f
        lf_row = jnp.minimum(f_raw, 0.0) - jnp.log1p(jnp.exp(-jnp.abs(f_raw)))
        b_row = _lane_cumsum(lf_row)
        b_last = jnp.min(b_row, axis=1, keepdims=True)
        bm = jnp.broadcast_to(b_row, (L, L))
        im = jnp.broadcast_to(i_row, (L, L))
        bt = bm.T
        it = im.T
        m_prev = m_ref[0:1, :]
        d = jnp.where(causal, bt - bm + im, NEG_BIG)
        a = bt + m_prev
        m_t = jnp.maximum(a, jnp.max(d, axis=1, keepdims=True))
        w_intra = jnp.exp(d - m_t)
        w_inter = jnp.exp(a - m_t)
        sc = lax.dot_general(q, k, (((1,), (1,)), ((), ())), preferred_element_type=F32) * w_intra
        num = (jnp.dot(sc.astype(BF16), v, preferred_element_type=F32)
               + _lane_tile(w_inter, dv // LANES)
               * jnp.dot(q, c_ref[...].astype(BF16), preferred_element_type=F32))
        qn = jnp.sum(q.astype(F32) * n_ref[0:1, :], axis=1, keepdims=True)
        den = jnp.sum(sc, axis=1, keepdims=True) + w_inter * qn
        den = jnp.maximum(jnp.abs(den), jnp.exp(-m_t))
        hc = num / _lane_tile(den, dv // LANES)
        hc = hc * lax.rsqrt(jnp.mean(hc * hc, axis=1, keepdims=True) + NORM_EPS) * wn_ref[...]
        out_ref[rows, :] = (jax.nn.sigmoid(og_ref[rows, :].astype(F32)) * hc).astype(out_ref.dtype)

        g = b_last - bt + it
        m_new = jnp.maximum(b_last + m_prev, jnp.max(g, axis=0, keepdims=True))
        ws = jnp.exp(g - m_new)
        decay = jnp.exp(b_last + m_prev - m_new)
        kw = k.astype(F32) * _lane_tile(ws, dk // LANES)
        c_ref[...] = (_lane_tile(decay, dv // LANES) * c_ref[...]
                      + jnp.dot(kw.T.astype(BF16), v, preferred_element_type=F32))
        n_ref[...] = (_lane_tile(decay, dk // LANES) * n_ref[...]
                      + jnp.sum(kw, axis=0, keepdims=True))
        m_ref[...] = jnp.broadcast_to(m_new, m_ref.shape)


def mlstm_core(proj, gates_t, bsz, b_gates, w_norm, t_rows=512):
    m = proj.shape[0]
    s = m // bsz
    n_heads = b_gates.shape[0] // 2
    dv = w_norm.shape[0] // n_heads
    dk = dv // 2
    t_rows = min(t_rows, s)
    assert s % t_rows == 0 and t_rows % LANES == 0 and dk % LANES == 0
    nt = s // t_rows
    bg = jnp.broadcast_to(b_gates.astype(F32)[:, None, None], (2 * n_heads, 1, LANES))
    row = lambda b, h, t: b * nt + t
    return pl.pallas_call(
        functools.partial(_mlstm_kernel, n_chunks=t_rows // LANES),
        grid=(bsz, n_heads, nt),
        in_specs=[pl.BlockSpec((t_rows, dk), lambda b, h, t: (row(b, h, t), h)),
                  pl.BlockSpec((t_rows, dk), lambda b, h, t: (row(b, h, t), n_heads + h)),
                  pl.BlockSpec((t_rows, dv), lambda b, h, t: (row(b, h, t), n_heads + h)),
                  pl.BlockSpec((t_rows, dv), lambda b, h, t: (row(b, h, t), 2 * n_heads + h)),
                  pl.BlockSpec((1, 1, t_rows), lambda b, h, t: (h, 0, row(b, h, t))),
                  pl.BlockSpec((1, 1, t_rows), lambda b, h, t: (n_heads + h, 0, row(b, h, t))),
                  pl.BlockSpec((1, 1, LANES), lambda b, h, t: (h, 0, 0)),
                  pl.BlockSpec((1, 1, LANES), lambda b, h, t: (n_heads + h, 0, 0)),
                  pl.BlockSpec((1, dv), lambda b, h, t: (0, h))],
        out_specs=pl.BlockSpec((t_rows, dv), lambda b, h, t: (row(b, h, t), h)),
        out_shape=jax.ShapeDtypeStruct((m, n_heads * dv), BF16),
        scratch_shapes=[pltpu.VMEM((dk, dv), F32), pltpu.VMEM((SUBLANES, dk), F32),
                        pltpu.VMEM((SUBLANES, LANES), F32)],
        compiler_params=_cparams("parallel", "parallel", "arbitrary"),
        name="mlstm_core",
    )(proj, proj, proj, proj, gates_t, gates_t, bg, bg, w_norm.reshape(1, n_heads * dv).astype(F32))


def mlstm_mixer(hn, bsz, w_in, b_gates, w_norm, w_out, x_res):
    n_heads = b_gates.shape[0] // 2
    n_main = w_in.shape[1] - 2 * n_heads
    w_g = jnp.pad(w_in[:, n_main:], ((0, 0), (0, LANES - 2 * n_heads))).astype(BF16)
    proj = matmul(hn, w_in.astype(BF16), out_dtype=BF16, tn=1024, n_out=n_main)
    gates_t = matmul(hn, w_g, tn=LANES)[:, :2 * n_heads].T[:, None, :]
    hh = mlstm_core(proj, gates_t, bsz, b_gates, w_norm)
    return matmul(hh, w_out.astype(BF16), residual=x_res)


def _diff_attn_kernel(q_ref, k_ref, v_ref, sl_ref, lq1_ref, lk1_ref, lq2_ref, lk2_ref, g_ref, o_ref,
                      m_ref, acc_ref, *, tq, tk, lambda_init):
    qi = pl.program_id(2)
    dh = DIFF_DH
    q = (q_ref[...].astype(F32) * (dh ** -0.5 * LOG2E)).astype(BF16)
    slope_row = _lane_tile(sl_ref[0], tk // LANES)
    col = lax.broadcasted_iota(jnp.int32, (1, tk), 1).astype(F32)
    m_ref[...] = jnp.full(m_ref.shape, NEG_BIG, F32)
    acc_ref[...] = jnp.zeros(acc_ref.shape, F32)

    def step(kk, masked):
        start = pl.multiple_of(kk * tk, tk)
        cb = (col + (kk * tk - qi * tq).astype(F32)) * slope_row
        kblk = k_ref[pl.ds(start, tk), :]
        vblk = jnp.concatenate([v_ref[pl.ds(start, tk), :], jnp.ones((tk, LANES), BF16)], axis=1)
        for c in range(2):
            s = lax.dot_general(q[:, c * dh:(c + 1) * dh], kblk[:, c * dh:(c + 1) * dh],
                                (((1,), (1,)), ((), ())), preferred_element_type=F32)
            s = s + cb
            if masked:
                row_i = lax.broadcasted_iota(jnp.int32, (tq, tk), 0)
                col_i = lax.broadcasted_iota(jnp.int32, (tq, tk), 1)
                s = jnp.where(col_i <= row_i, s, NEG_BIG)
            m_prev = m_ref[c]
            m_new = jnp.maximum(m_prev, jnp.max(s, axis=1, keepdims=True))
            alpha = jnp.exp2(m_prev - m_new)
            p = jnp.exp2(s - _lane_tile(m_new, tk // LANES))
            m_ref[c] = m_new
            acc_ref[c] = (acc_ref[c] * _lane_tile(alpha, 2 * dh // LANES + 1)
                          + jnp.dot(p.astype(BF16), vblk, preferred_element_type=F32))

    def body(kk, carry):
        step(kk, False)
        return carry

    lax.fori_loop(0, qi, body, 0)
    step(qi, True)

    lam = (jnp.exp(jnp.sum(lq1_ref[...] * lk1_ref[...])) - jnp.exp(jnp.sum(lq2_ref[...] * lk2_ref[...]))
           + lambda_init)
    rep = 2 * dh // LANES
    o = (acc_ref[0, :, :2 * dh] / _lane_tile(acc_ref[0, :, 2 * dh:], rep)
         - lam * (acc_ref[1, :, :2 * dh] / _lane_tile(acc_ref[1, :, 2 * dh:], rep)))
    o = o * lax.rsqrt(jnp.mean(o * o, axis=-1, keepdims=True) + NORM_EPS) * g_ref[...]
    o_ref[...] = (o * (1.0 - lambda_init)).astype(o_ref.dtype)


def diff_attn_core(proj, bsz, lam_q1, lam_k1, lam_q2, lam_k2, w_subln, lambda_init, tq=1024):
    m, d3 = proj.shape
    d = d3 // 3
    s = m // bsz
    hw = 2 * DIFF_DH
    n_heads = d // hw
    tq = min(tq, s)
    assert s % tq == 0
    nq = s // tq
    slopes = jnp.exp2(-8.0 * jnp.arange(1, n_heads + 1, dtype=F32) / n_heads) * LOG2E
    slopes = jnp.broadcast_to(slopes[:, None, None], (n_heads, 1, LANES))
    vec = lambda a: a.reshape(1, -1).astype(F32)
    small = pl.BlockSpec((1, DIFF_DH), lambda b, h, i: (0, 0))
    return pl.pallas_call(
        functools.partial(_diff_attn_kernel, tq=tq, tk=tq, lambda_init=lambda_init),
        grid=(bsz, n_heads, nq),
        in_specs=[pl.BlockSpec((tq, hw), lambda b, h, i: (b * nq + i, h)),
                  pl.BlockSpec((s, hw), lambda b, h, i: (b, n_heads + h)),
                  pl.BlockSpec((s, hw), lambda b, h, i: (b, 2 * n_heads + h)),
                  pl.BlockSpec((1, 1, LANES), lambda b, h, i: (h, 0, 0)),
                  small, small, small, small,
                  pl.BlockSpec((1, hw), lambda b, h, i: (0, 0))],
        out_specs=pl.BlockSpec((tq, hw), lambda b, h, i: (b * nq + i, h)),
        out_shape=jax.ShapeDtypeStruct((m, d), BF16),
        scratch_shapes=[pltpu.VMEM((2, tq, LANES), F32), pltpu.VMEM((2, tq, hw + LANES), F32)],
        compiler_params=_cparams("parallel", "parallel", "arbitrary"),
        name="diff_attn",
    )(proj, proj, proj, slopes, vec(lam_q1), vec(lam_k1), vec(lam_q2), vec(lam_k2), vec(w_subln))


def diff_attn_mixer(hn, bsz, w_in, lam_q1, lam_k1, lam_q2, lam_k2, w_subln, w_out, lambda_init, x_res):
    proj = matmul(hn, w_in.astype(BF16), out_dtype=BF16, tn=1024)
    o = diff_attn_core(proj, bsz, lam_q1, lam_k1, lam_q2, lam_k2, w_subln, lambda_init)
    return matmul(o, w_out.astype(BF16), residual=x_res)


def _dwconv_silu_kernel(u_ref, w_ref, b_ref, o_ref, tail_ref, *, width):
    @pl.when(pl.program_id(2) == 0)
    def _():
        tail_ref[...] = jnp.zeros_like(tail_ref)

    u = u_ref[...].astype(F32)
    tail = tail_ref[...]
    y = w_ref[width - 1:width, :] * u + b_ref[...]
    for j in range(width - 1):
        y = y + w_ref[j:j + 1, :] * _shift_rows(u, tail, width - 1 - j)
    tail_ref[...] = u[-SUBLANES:]
    o_ref[...] = (y * jax.nn.sigmoid(y)).astype(o_ref.dtype)


def dwconv_silu(u, col0, conv_w, conv_b, bsz, ts=512, tc=512):
    m = u.shape[0]
    width, ch = conv_w.shape
    s = m // bsz
    ts = _pick_tile(s, ts, SUBLANES)
    tc = _pick_tile(math.gcd(ch, col0) if col0 else ch, tc, LANES)
    nsb = s // ts
    c0 = col0 // tc
    return pl.pallas_call(
        functools.partial(_dwconv_silu_kernel, width=width),
        grid=(ch // tc, bsz, nsb),
        in_specs=[pl.BlockSpec((ts, tc), lambda c, b, t: (b * nsb + t, c0 + c)),
                  pl.BlockSpec((width, tc), lambda c, b, t: (0, c)),
                  pl.BlockSpec((1, tc), lambda c, b, t: (0, c))],
        out_specs=pl.BlockSpec((ts, tc), lambda c, b, t: (b * nsb + t, c)),
        out_shape=jax.ShapeDtypeStruct((m, ch), BF16),
        scratch_shapes=[pltpu.VMEM((SUBLANES, tc), F32)],
        compiler_params=_cparams("parallel", "parallel", "arbitrary"),
        name="dwconv_silu",
    )(u, conv_w.astype(F32), conv_b.reshape(1, ch).astype(F32))


def _split3(v):
    hi = v.astype(BF16)
    r = v - hi.astype(F32)
    mid = r.astype(BF16)
    lo = (r - mid.astype(F32)).astype(BF16)
    return hi, mid, lo


def _dot_exact_rhs(lhs_f32, rhs_bf16):
    n = lhs_f32.shape[0]
    out = jnp.dot(jnp.concatenate(_split3(lhs_f32), axis=0), rhs_bf16, preferred_element_type=F32)
    return out[:n] + out[n:2 * n] + out[2 * n:]


def _dot_exact_lhs(lhs_bf16, rhs_f32):
    hi, mid, lo = _split3(rhs_f32)
    return (jnp.dot(lhs_bf16, hi, preferred_element_type=F32) + jnp.dot(lhs_bf16, mid, preferred_element_type=F32)
            + jnp.dot(lhs_bf16, lo, preferred_element_type=F32))


def _softplus(x):
    return jnp.maximum(x, 0.0) + jnp.log1p(jnp.exp(-jnp.abs(x)))


def _ssd_kernel(x_ref, b_ref, c_ref, z_ref, dtc_ref, dtp_ref, dtbc_ref, alc_ref, dtbp_ref, alp_ref, dsk_ref,
                wn_ref, ep_ref, el_ref, o_ref, st_ref, *, L, n_chunks, hpg, p):
    @pl.when(pl.program_id(2) == 0)
    def _():
        st_ref[...] = jnp.zeros_like(st_ref)

    npairs = hpg // 2
    a_c = -jnp.exp(alc_ref[0])
    a_p = -jnp.exp(alp_ref[0])
    tri = (lax.broadcasted_iota(jnp.int32, (L, L), 1) <= lax.broadcasted_iota(jnp.int32, (L, L), 0)).astype(BF16)
    r2 = lax.broadcasted_iota(jnp.int32, (2 * L, 2 * L), 0)
    c2 = lax.broadcasted_iota(jnp.int32, (2 * L, 2 * L), 1)
    bd = ((r2 <= c2) & ((r2 < L) == (c2 < L))).astype(BF16)
    lane2 = lax.broadcasted_iota(jnp.int32, (L, 2 * L), 1)
    row2 = lax.broadcasted_iota(jnp.int32, (L, 2 * L), 0)
    tri2 = jnp.where(lane2 < L, lane2, lane2 - L) <= row2
    lane_p = lax.broadcasted_iota(jnp.int32, (L, 2 * p), 1)

    def chunk(c, carry):
        r0 = pl.multiple_of(c * L, L)
        x = x_ref[pl.ds(r0, L), :].astype(F32)
        bmat = b_ref[pl.ds(r0, L), :]
        cmat = c_ref[pl.ds(r0, L), :]
        dt_c = _softplus(dtc_ref[0, pl.ds(r0, L), :] + dtbc_ref[0])
        cum_c = _dot_exact_lhs(tri, dt_c * a_c)
        p0 = pl.multiple_of(c * npairs, npairs)
        dt_p = _softplus(dtp_ref[0, pl.ds(p0, npairs), :] + dtbp_ref[0])
        cum_p = _dot_exact_rhs(dt_p * a_p, bd)
        both = _dot_exact_rhs(jnp.concatenate([cum_c, dt_c], axis=0), ep_ref[...])
        cum_e, dt_e = both[:L], both[L:]
        cum_l = cum_e if el_ref is None else _dot_exact_rhs(cum_c, el_ref[...])
        cb = lax.dot_general(cmat, bmat, (((1,), (1,)), ((), ())), preferred_element_type=F32)
        cb2 = jnp.concatenate([cb, cb], axis=1)
        x16 = x.astype(BF16)
        ys = []
        for j in range(npairs):
            seg = cum_l[:, j * 2 * L:(j + 1) * 2 * L] - cum_p[j:j + 1, :]
            w = jnp.exp(jnp.where(tri2, seg, NEG_BIG)) * cb2 * dt_p[j:j + 1, :]
            xp = x16[:, j * 2 * p:(j + 1) * 2 * p]
            rhs = jnp.concatenate([jnp.where(lane_p < p, xp, 0), jnp.where(lane_p >= p, xp, 0)], axis=0)
            ys.append(jnp.dot(w.astype(BF16), rhs, preferred_element_type=F32))
        y = jnp.concatenate(ys, axis=1)
        st = st_ref[...]
        y = y + jnp.dot(cmat, st.astype(BF16), preferred_element_type=F32) * jnp.exp(cum_e)
        cum_last = cum_e[L - 1:L, :]
        xw = (x * (jnp.exp(cum_last - cum_e) * dt_e)).astype(BF16)
        st_ref[...] = jnp.exp(cum_last) * st + jnp.dot(bmat.astype(F32).T.astype(BF16), xw,
                                                       preferred_element_type=F32)
        y = y + x * dsk_ref[0]
        zf = z_ref[pl.ds(r0, L), :].astype(F32)
        y = y * (zf * jax.nn.sigmoid(zf))
        y = y * lax.rsqrt(jnp.mean(y * y, axis=1, keepdims=True) + NORM_EPS) * wn_ref[0]
        o_ref[pl.ds(r0, L), :] = y.astype(o_ref.dtype)
        return carry

    lax.fori_loop(0, n_chunks, chunk, 0, unroll=8)


def _kernel_no_el(*refs, **kw):
    *ins, o_ref, st_ref = refs
    _ssd_kernel(*ins, None, o_ref, st_ref, **kw)


def ssd_core(proj, xbc, dt_raw, bsz, dt_bias, a_log, d_skip, w_norm, t_rows=512):
    m = proj.shape[0]
    s = m // bsz
    nh = dt_bias.shape[0]
    p, n, g, L = SSM_HEADDIM, SSM_STATE, SSM_GROUPS, SSM_CHUNK
    di = nh * p
    hpg = nh // g
    gw = hpg * p
    t_rows = min(t_rows, s)
    assert s % t_rows == 0 and t_rows % L == 0 and hpg % 2 == 0 and 2 * L == LANES
    nt = s // t_rows
    n_chunks = t_rows // L
    npairs = hpg // 2
    nc_all = m // L
    dt_c = dt_raw.reshape(m, g, hpg).transpose(1, 0, 2)
    dt_p = (dt_raw.reshape(nc_all, L, g, npairs, 2).transpose(2, 0, 3, 4, 1)
            .reshape(g, nc_all * npairs, 2 * L))
    col = lambda a: a.astype(F32).reshape(g, 1, hpg)
    pair = lambda a: jnp.broadcast_to(a.astype(F32).reshape(g, npairs, 2, 1), (g, npairs, 2, L)).reshape(g, npairs, 2 * L)
    expand = lambda rep: jnp.repeat(jnp.eye(hpg, dtype=BF16), rep, axis=1)
    d_e = jnp.repeat(d_skip.astype(F32), p).reshape(g, 1, gw)
    same = p == L
    row = lambda b, gg, t: b * nt + t
    gspec = lambda shape: pl.BlockSpec((1,) + shape, lambda b, gg, t: (gg, 0, 0))
    in_specs = [pl.BlockSpec((t_rows, gw), lambda b, gg, t: (row(b, gg, t), gg)),
                pl.BlockSpec((t_rows, n), lambda b, gg, t: (row(b, gg, t), di // n + gg)),
                pl.BlockSpec((t_rows, n), lambda b, gg, t: (row(b, gg, t), di // n + g + gg)),
                pl.BlockSpec((t_rows, gw), lambda b, gg, t: (row(b, gg, t), gg)),
                pl.BlockSpec((1, t_rows, hpg), lambda b, gg, t: (gg, row(b, gg, t), 0)),
                pl.BlockSpec((1, n_chunks * npairs, 2 * L), lambda b, gg, t: (gg, row(b, gg, t), 0)),
                gspec((1, hpg)), gspec((1, hpg)), gspec((npairs, 2 * L)), gspec((npairs, 2 * L)),
                gspec((1, gw)), gspec((1, gw)),
                pl.BlockSpec((hpg, gw), lambda b, gg, t: (0, 0))]
    args = [xbc, xbc, xbc, proj, dt_c, dt_p, col(dt_bias), col(a_log), pair(dt_bias), pair(a_log), d_e,
            w_norm.astype(F32).reshape(g, 1, gw), expand(p)]
    if not same:
        in_specs.append(pl.BlockSpec((hpg, hpg * L), lambda b, gg, t: (0, 0)))
        args.append(expand(L))
    kern = functools.partial(_ssd_kernel if not same else _kernel_no_el, L=L, n_chunks=n_chunks, hpg=hpg, p=p)
    return pl.pallas_call(
        kern,
        grid=(bsz, g, nt),
        in_specs=in_specs,
        out_specs=pl.BlockSpec((t_rows, gw), lambda b, gg, t: (row(b, gg, t), gg)),
        out_shape=jax.ShapeDtypeStruct((m, di), BF16),
        scratch_shapes=[pltpu.VMEM((n, gw), F32)],
        compiler_params=_cparams("parallel", "parallel", "arbitrary"),
        name="ssd_core",
    )(*args)


def mamba2_mixer(hn, bsz, w_in, conv_w, conv_b, dt_bias, a_log, d_skip, w_norm, w_out, x_res):
    nh = dt_bias.shape[0]
    di = nh * SSM_HEADDIM
    n_main = w_in.shape[1] - nh
    w_bf = w_in.astype(BF16)
    proj = matmul(hn, w_bf, out_dtype=BF16, tn=1024, n_out=n_main)
    dt_raw = matmul(hn, w_bf, tn=LANES, col0=n_main, n_out=nh)
    xbc = dwconv_silu(proj, di, conv_w, conv_b, bsz)
    y = ssd_core(proj, xbc, dt_raw, bsz, dt_bias, a_log, d_skip, w_norm)
    return matmul(y, w_out.astype(BF16), residual=x_res, tm=512, tk=w_out.shape[0])


def _hgrn2_kernel(q_ref, f_ref, i_ref, g_ref, lb_ref, wn_ref, o_ref, st_ref, *, L, sub, n_chunks, layer):
    @pl.when(pl.program_id(2) == 0)
    def _():
        st_ref[...] = jnp.zeros_like(st_ref)

    dk = q_ref.shape[1]
    lbp = lb_ref[...]
    e = jnp.exp(lbp - jnp.max(lbp, axis=0, keepdims=True))
    sm = e / jnp.sum(e, axis=0, keepdims=True)
    lb = jnp.sum(sm[1:layer + 1], axis=0, keepdims=True) if layer > 0 else jnp.zeros((1, dk), F32)
    log_lb = jnp.log(lb)
    log_1mlb = jnp.log1p(-lb)
    tri = (lax.broadcasted_iota(jnp.int32, (L, L), 1) <= lax.broadcasted_iota(jnp.int32, (L, L), 0)).astype(BF16)
    sub_row = lax.broadcasted_iota(jnp.int32, (sub, dk), 0)
    lane_s = lax.broadcasted_iota(jnp.int32, (sub, sub), 1)
    nsub = L // sub

    def chunk(c, carry):
        r0 = pl.multiple_of(c * L, L)
        qf = q_ref[pl.ds(r0, L), :].astype(F32)
        qa = qf * jax.nn.sigmoid(qf)
        fx = f_ref[pl.ds(r0, L), :].astype(F32)
        key = (1.0 - lb) * jax.nn.sigmoid(-fx)
        log_f = jnp.logaddexp(log_lb, log_1mlb + jax.nn.log_sigmoid(fx))
        b = _dot_exact_lhs(tri, log_f)
        v = i_ref[pl.ds(r0, L), :]
        strips = []
        for blk in range(nsub):
            t0 = blk * sub
            bq = b[t0:t0 + sub]
            qb = qa[t0:t0 + sub]
            cols = []
            if blk > 0:
                ref_row = b[t0 - 1:t0]
                qd = (qb * jnp.exp(bq - ref_row)).astype(BF16)
                kd = (key[:t0] * jnp.exp(ref_row - b[:t0])).astype(BF16)
                cols.append(lax.dot_general(qd, kd, (((1,), (1,)), ((), ())), preferred_element_type=F32))
            diag = jnp.zeros((sub, sub), F32)
            for sl in range(sub):
                srow = t0 + sl
                dec = jnp.exp(jnp.where(sub_row >= sl, bq - b[srow:srow + 1], NEG_BIG))
                col = jnp.sum(qb * dec * key[srow:srow + 1], axis=1, keepdims=True)
                diag = jnp.where(lane_s == sl, col, diag)
            cols.append(diag)
            if t0 + sub < L:
                cols.append(jnp.zeros((sub, L - t0 - sub), F32))
            strips.append(jnp.concatenate(cols, axis=1))
        attn = jnp.concatenate(strips, axis=0)
        st = st_ref[...]
        o = (jnp.dot(attn.astype(BF16), v, preferred_element_type=F32)
             + lax.dot_general((qa * jnp.exp(b)).astype(BF16), st.astype(BF16), (((1,), (1,)), ((), ())),
                               preferred_element_type=F32))
        b_last = b[L - 1:L]
        kd = (key * jnp.exp(b_last - b)).astype(BF16)
        st_ref[...] = jnp.exp(b_last) * st + jnp.dot(v.astype(F32).T.astype(BF16), kd, preferred_element_type=F32)
        o = o * lax.rsqrt(jnp.mean(o * o, axis=1, keepdims=True) + NORM_EPS) * wn_ref[...]
        gf = g_ref[pl.ds(r0, L), :].astype(F32)
        o_ref[pl.ds(r0, L), :] = (o * (gf * jax.nn.sigmoid(gf))).astype(o_ref.dtype)
        return carry

    lax.fori_loop(0, n_chunks, chunk, 0, unroll=16)


def hgrn2_core(proj, bsz, lb_params, layer, w_gnorm, t_rows=1024):
    m, d4 = proj.shape
    d = d4 // 4
    s = m // bsz
    dk, L = HGRN_DK, HGRN_CHUNK
    n_heads = d // dk
    assert d // n_heads == dk, "kernel assumes DV == DK"
    t_rows = min(t_rows, s)
    assert s % t_rows == 0 and t_rows % L == 0
    nt = s // t_rows
    depth = lb_params.shape[0]
    sec = lambda k: pl.BlockSpec((t_rows, dk), lambda b, h, t: (b * nt + t, k * n_heads + h))
    return pl.pallas_call(
        functools.partial(_hgrn2_kernel, L=L, sub=SUBLANES, n_chunks=t_rows // L, layer=layer),
        grid=(bsz, n_heads, nt),
        in_specs=[sec(0), sec(1), sec(2), sec(3),
                  pl.BlockSpec((depth, dk), lambda b, h, t: (0, h)),
                  pl.BlockSpec((1, dk), lambda b, h, t: (0, 0))],
        out_specs=pl.BlockSpec((t_rows, dk), lambda b, h, t: (b * nt + t, h)),
        out_shape=jax.ShapeDtypeStruct((m, d), BF16),
        scratch_shapes=[pltpu.VMEM((dk, dk), F32)],
        compiler_params=_cparams("parallel", "parallel", "arbitrary"),
        name="hgrn2_core",
    )(proj, proj, proj, proj, lb_params.astype(F32), w_gnorm.reshape(1, dk).astype(F32))


def hgrn2_mixer(hn, bsz, w_in, lb_params, layer, w_gnorm, w_out, x_res):
    proj = matmul(hn, w_in.astype(BF16), out_dtype=BF16, tn=1024)
    o = hgrn2_core(proj, bsz, lb_params, layer, w_gnorm)
    return matmul(o, w_out.astype(BF16), residual=x_res)


def conv_ffn(x2d, bsz, g_norm, w_up, conv_w, conv_b, w_down):
    hn = rmsnorm(x2d, g_norm)
    a = ffn_up_conv_act(hn, w_up.astype(BF16), conv_w, conv_b, bsz)
    return matmul(a, w_down.astype(BF16), residual=x2d, tm=512, tk=w_down.shape[0])


def kernel(x, norm_mix, norm_ffn, norm_final, mlstm_w_in, mlstm_b_gates, mlstm_w_norm, mlstm_w_out, diff_w_in, diff_lam_q1, diff_lam_k1, diff_lam_q2, diff_lam_k2, diff_w_subln, diff_w_out, ssm_w_in, ssm_conv_w, ssm_conv_b, ssm_dt_bias, ssm_a_log, ssm_d, ssm_w_norm, ssm_w_out, hgrn_w_in, hgrn_lb, hgrn_w_gnorm, hgrn_w_out, ffn_w_up, ffn_conv_w, ffn_conv_b, ffn_w_down):
    bsz, s, d = x.shape
    depth = norm_mix.shape[0]
    x2d = x.reshape(bsz * s, d)
    for layer in range(depth):
        kind, j = layer % N_MIXERS, layer // N_MIXERS
        hn = rmsnorm(x2d, norm_mix[layer])
        if kind == 0:
            x2d = mlstm_mixer(hn, bsz, mlstm_w_in[j], mlstm_b_gates[j], mlstm_w_norm[j], mlstm_w_out[j], x2d)
        elif kind == 1:
            lambda_init = 0.8 - 0.6 * math.exp(-0.3 * layer)
            x2d = diff_attn_mixer(hn, bsz, diff_w_in[j], diff_lam_q1[j], diff_lam_k1[j], diff_lam_q2[j],
                                  diff_lam_k2[j], diff_w_subln[j], diff_w_out[j], lambda_init, x2d)
        elif kind == 2:
            x2d = mamba2_mixer(hn, bsz, ssm_w_in[j], ssm_conv_w[j], ssm_conv_b[j], ssm_dt_bias[j],
                               ssm_a_log[j], ssm_d[j], ssm_w_norm[j], ssm_w_out[j], x2d)
        else:
            x2d = hgrn2_mixer(hn, bsz, hgrn_w_in[j], hgrn_lb, layer, hgrn_w_gnorm[j], hgrn_w_out[j], x2d)
        x2d = conv_ffn(x2d, bsz, norm_ffn[layer], ffn_w_up[layer], ffn_conv_w[layer], ffn_conv_b[layer],
                       ffn_w_down[layer])
    return rmsnorm(x2d, norm_final, out_dtype=x.dtype).reshape(bsz, s, d)
```

```python
import functools
import math

import jax
import jax.numpy as jnp
from jax import lax
from jax.experimental import pallas as pl
from jax.experimental.pallas import tpu as pltpu

NORM_EPS = 1e-6
N_MIXERS = 4
DIFF_DH = 128
SSM_HEADDIM = 64
SSM_STATE = 128
SSM_GROUPS = 8
SSM_CHUNK = 64
HGRN_DK = 128
HGRN_CHUNK = 64
LOG2E = 1.4426950408889634
NEG_BIG = -1e30

V7X_VMEM_BYTES = 64 * 1024 * 1024
VMEM_LIMIT_BYTES = 56 * 1024 * 1024
LANES = 128
SUBLANES = 8

F32 = jnp.float32
BF16 = jnp.bfloat16


def _cparams(*sem):
    return pltpu.CompilerParams(dimension_semantics=sem, vmem_limit_bytes=VMEM_LIMIT_BYTES)


def _lane_tile(x, n):
    return x if n == 1 else jnp.concatenate([x] * n, axis=1)


def _mm_kernel(x_ref, w_ref, *rest, has_res, nk):
    if has_res:
        r_ref, o_ref = rest
    else:
        (o_ref,) = rest
    part = jnp.dot(x_ref[...], w_ref[...], preferred_element_type=F32)
    if nk == 1:
        if has_res:
            part = part + r_ref[...]
        o_ref[...] = part.astype(o_ref.dtype)
    else:
        k = pl.program_id(2)

        @pl.when(k == 0)
        def _():
            if has_res:
                o_ref[...] = r_ref[...] + part
            else:
                o_ref[...] = part

        @pl.when(k > 0)
        def _():
            o_ref[...] += part


def _pick_tile(n, target, quantum):
    best = None
    t = quantum
    while t <= min(n, target):
        if n % t == 0:
            best = t
        t += quantum
    assert best is not None, (n, target, quantum)
    return best


def matmul(x, w, residual=None, out_dtype=F32, tm=1024, tn=512, tk=4096, col0=0, n_out=None):
    m, k = x.shape
    k2 = w.shape[0]
    n = w.shape[1] - col0 if n_out is None else n_out
    assert k == k2
    tm = _pick_tile(m, tm, SUBLANES)
    tn = _pick_tile(math.gcd(n, col0) if col0 else n, tn, LANES)
    tk = k if k <= tk else _pick_tile(k, max(tk, 5504), LANES)
    nk = k // tk
    if nk > 1:
        assert out_dtype == F32
    has_res = residual is not None
    jb0 = col0 // tn
    in_specs = [pl.BlockSpec((tm, tk), lambda i, j, kk: (i, kk)),
                pl.BlockSpec((tk, tn), lambda i, j, kk: (kk, jb0 + j))]
    args = [x, w]
    if has_res:
        in_specs.append(pl.BlockSpec((tm, tn), lambda i, j, kk: (i, j)))
        args.append(residual)
    return pl.pallas_call(
        functools.partial(_mm_kernel, has_res=has_res, nk=nk),
        grid=(m // tm, n // tn, nk),
        in_specs=in_specs,
        out_specs=pl.BlockSpec((tm, tn), lambda i, j, kk: (i, j)),
        out_shape=jax.ShapeDtypeStruct((m, n), out_dtype),
        compiler_params=_cparams("parallel", "parallel", "arbitrary"),
        name="matmul",
    )(*args)


def _rmsnorm_kernel(x_ref, g_ref, o_ref):
    x = x_ref[...].astype(F32)
    ms = jnp.mean(x * x, axis=-1, keepdims=True)
    o_ref[...] = (x * lax.rsqrt(ms + NORM_EPS) * g_ref[...]).astype(o_ref.dtype)


def rmsnorm(x, g, out_dtype=BF16, tm=512):
    m, d = x.shape
    tm = _pick_tile(m, tm, SUBLANES)
    return pl.pallas_call(
        _rmsnorm_kernel,
        grid=(m // tm,),
        in_specs=[pl.BlockSpec((tm, d), lambda i: (i, 0)),
                  pl.BlockSpec((1, d), lambda i: (0, 0))],
        out_specs=pl.BlockSpec((tm, d), lambda i: (i, 0)),
        out_shape=jax.ShapeDtypeStruct((m, d), out_dtype),
        compiler_params=_cparams("parallel"),
        name="rmsnorm",
    )(x, g.reshape(1, d).astype(F32))


def _shift_rows(cur, prev_tail, shift):
    rolled = pltpu.roll(cur, shift, axis=0)
    tail = pltpu.roll(prev_tail, shift, axis=0)
    row = lax.broadcasted_iota(jnp.int32, (SUBLANES, cur.shape[1]), 0)
    head = jnp.where(row < shift, tail, rolled[:SUBLANES])
    return jnp.concatenate([head, rolled[SUBLANES:]], axis=0)


def _ffn_up_kernel(x_ref, wg_ref, wv_ref, cg_ref, cv_ref, bg_ref, bv_ref, o_ref,
                   ug_ref, uv_ref, tg_ref, tv_ref, *, tiles_per_seq, ncb):
    i = pl.program_id(0)
    j = pl.program_id(1)

    @pl.when((i == 0) & (j == 0))
    def _():
        for r in (ug_ref, uv_ref, tg_ref, tv_ref):
            r[...] = jnp.zeros_like(r)

    x = x_ref[...]
    ug_new = jnp.dot(x, wg_ref[...], preferred_element_type=F32)
    uv_new = jnp.dot(x, wv_ref[...], preferred_element_type=F32)
    jm = jnp.maximum(j - 1, 0)
    seq_start = i % tiles_per_seq == 0

    def conv(u, t_ref, w_ref, b_ref):
        old = t_ref[jm]
        tail = jnp.where(seq_start, 0.0, old)
        y = (w_ref[2:3, :] * u + w_ref[1:2, :] * _shift_rows(u, tail, 1)
             + w_ref[0:1, :] * _shift_rows(u, tail, 2) + b_ref[...])
        t_ref[jm] = jnp.where(j > 0, u[-SUBLANES:], old)
        return y

    g = conv(ug_ref[...], tg_ref, cg_ref, bg_ref)
    v = conv(uv_ref[...], tv_ref, cv_ref, bv_ref)
    o_ref[...] = (g * jax.nn.sigmoid(g) * v).astype(o_ref.dtype)
    ug_ref[...] = ug_new
    uv_ref[...] = uv_new


def ffn_up_conv_act(hn, w_up, conv_w, conv_b, bsz, tm=1024, tn=256):
    m, k = hn.shape
    f = w_up.shape[1] // 2
    s = m // bsz
    tm = _pick_tile(s, tm, SUBLANES)
    tn = _pick_tile(f, tn, LANES)
    ncb = f // tn
    width = conv_w.shape[0]
    assert width == 3
    cw = conv_w.astype(F32)
    cb = conv_b.reshape(1, 2 * f).astype(F32)
    cur = lambda i, j: (0, jnp.minimum(j, ncb - 1))
    cur_v = lambda i, j: (0, jnp.minimum(j, ncb - 1) + ncb)
    prev = lambda i, j: (0, jnp.maximum(j - 1, 0))
    prev_v = lambda i, j: (0, jnp.maximum(j - 1, 0) + ncb)
    return pl.pallas_call(
        functools.partial(_ffn_up_kernel, tiles_per_seq=s // tm, ncb=ncb),
        grid=(m // tm, ncb + 1),
        in_specs=[pl.BlockSpec((tm, k), lambda i, j: (i, 0)),
                  pl.BlockSpec((k, tn), cur), pl.BlockSpec((k, tn), cur_v),
                  pl.BlockSpec((width, tn), prev), pl.BlockSpec((width, tn), prev_v),
                  pl.BlockSpec((1, tn), prev), pl.BlockSpec((1, tn), prev_v)],
        out_specs=pl.BlockSpec((tm, tn), lambda i, j: (i, jnp.maximum(j - 1, 0))),
        out_shape=jax.ShapeDtypeStruct((m, f), BF16),
        scratch_shapes=[pltpu.VMEM((tm, tn), F32), pltpu.VMEM((tm, tn), F32),
                        pltpu.VMEM((ncb, SUBLANES, tn), F32), pltpu.VMEM((ncb, SUBLANES, tn), F32)],
        compiler_params=_cparams("arbitrary", "arbitrary"),
        name="ffn_up_conv_act",
    )(hn, w_up, w_up, cw, cw, cb, cb)


def _lane_cumsum(x):
    lane = lax.broadcasted_iota(jnp.int32, x.shape, 1)
    shift = 1
    while shift < x.shape[1]:
        x = x + jnp.where(lane >= shift, pltpu.roll(x, shift, axis=1), 0.0)
        shift *= 2
    return x


def _mlstm_kernel(q_ref, k_ref, v_ref, og_ref, gi_ref, gf_ref, bi_ref, bf_ref, wn_ref, out_ref,
                  c_ref, n_ref, m_ref, *, n_chunks):
    L = LANES
    dk = q_ref.shape[1]
    dv = v_ref.shape[1]

    @pl.when(pl.program_id(2) == 0)
    def _():
        c_ref[...] = jnp.zeros_like(c_ref)
        n_ref[...] = jnp.zeros_like(n_ref)
        m_ref[...] = jnp.zeros_like(m_ref)

    row_i = lax.broadcasted_iota(jnp.int32, (L, L), 0)
    col_i = lax.broadcasted_iota(jnp.int32, (L, L), 1)
    causal = col_i <= row_i
    bias_i = bi_ref[0]
    bias_f = bf_ref[0]

    for c in range(n_chunks):
        rows = slice(c * L, (c + 1) * L)
        q = (q_ref[rows, :].astype(F32) * (dk ** -0.5)).astype(BF16)
        k = k_ref[rows, :]
        v = v_ref[rows, :]
        i_row = gi_ref[0, :, rows] + bias_i
        f_raw = gf_ref[0, :, rows] + bias_f
        lf_row = jnp.minimum(f_raw, 0.0) - jnp.log1p(jnp.exp(-jnp.abs(f_raw)))
        b_row = _lane_cumsum(lf_row)
        b_last = jnp.min(b_row, axis=1, keepdims=True)
        bm = jnp.broadcast_to(b_row, (L, L))
        im = jnp.broadcast_to(i_row, (L, L))
        bt = bm.T
        it = im.T
        m_prev = m_ref[0:1, :]
        d = jnp.where(causal, bt - bm + im, NEG_BIG)
        a = bt + m_prev
        m_t = jnp.maximum(a, jnp.max(d, axis=1, keepdims=True))
        w_intra = jnp.exp(d - m_t)
        w_inter = jnp.exp(a - m_t)
        sc = lax.dot_general(q, k, (((1,), (1,)), ((), ())), preferred_element_type=F32) * w_intra
        num = (jnp.dot(sc.astype(BF16), v, preferred_element_type=F32)
               + _lane_tile(w_inter, dv // LANES)
               * jnp.dot(q, c_ref[...].astype(BF16), preferred_element_type=F32))
        qn = jnp.sum(q.astype(F32) * n_ref[0:1, :], axis=1, keepdims=True)
        den = jnp.sum(sc, axis=1, keepdims=True) + w_inter * qn
        den = jnp.maximum(jnp.abs(den), jnp.exp(-m_t))
        hc = num / _lane_tile(den, dv // LANES)
        hc = hc * lax.rsqrt(jnp.mean(hc * hc, axis=1, keepdims=True) + NORM_EPS) * wn_ref[...]
        out_ref[rows, :] = (jax.nn.sigmoid(og_ref[rows, :].astype(F32)) * hc).astype(out_ref.dtype)

        g = b_last - bt + it
        m_new = jnp.maximum(b_last + m_prev, jnp.max(g, axis=0, keepdims=True))
        ws = jnp.exp(g - m_new)
        decay = jnp.exp(b_last + m_prev - m_new)
        kw = k.astype(F32) * _lane_tile(ws, dk // LANES)
        c_ref[...] = (_lane_tile(decay, dv // LANES) * c_ref[...]
                      + jnp.dot(kw.T.astype(BF16), v, preferred_element_type=F32))
        n_ref[...] = (_lane_tile(decay, dk // LANES) * n_ref[...]
                      + jnp.sum(kw, axis=0, keepdims=True))
        m_ref[...] = jnp.broadcast_to(m_new, m_ref.shape)


def mlstm_core(proj, gates_t, bsz, b_gates, w_norm, t_rows=512):
    m = proj.shape[0]
    s = m // bsz
    n_heads = b_gates.shape[0] // 2
    dv = w_norm.shape[0] // n_heads
    dk = dv // 2
    t_rows = min(t_rows, s)
    assert s % t_rows == 0 and t_rows % LANES == 0 and dk % LANES == 0
    nt = s // t_rows
    bg = jnp.broadcast_to(b_gates.astype(F32)[:, None, None], (2 * n_heads, 1, LANES))
    row = lambda b, h, t: b * nt + t
    return pl.pallas_call(
        functools.partial(_mlstm_kernel, n_chunks=t_rows // LANES),
        grid=(bsz, n_heads, nt),
        in_specs=[pl.BlockSpec((t_rows, dk), lambda b, h, t: (row(b, h, t), h)),
                  pl.BlockSpec((t_rows, dk), lambda b, h, t: (row(b, h, t), n_heads + h)),
                  pl.BlockSpec((t_rows, dv), lambda b, h, t: (row(b, h, t), n_heads + h)),
                  pl.BlockSpec((t_rows, dv), lambda b, h, t: (row(b, h, t), 2 * n_heads + h)),
                  pl.BlockSpec((1, 1, t_rows), lambda b, h, t: (h, 0, row(b, h, t))),
                  pl.BlockSpec((1, 1, t_rows), lambda b, h, t: (n_heads + h, 0, row(b, h, t))),
                  pl.BlockSpec((1, 1, LANES), lambda b, h, t: (h, 0, 0)),
                  pl.BlockSpec((1, 1, LANES), lambda b, h, t: (n_heads + h, 0, 0)),
                  pl.BlockSpec((1, dv), lambda b, h, t: (0, h))],
        out_specs=pl.BlockSpec((t_rows, dv), lambda b, h, t: (row(b, h, t), h)),
        out_shape=jax.ShapeDtypeStruct((m, n_heads * dv), BF16),
        scratch_shapes=[pltpu.VMEM((dk, dv), F32), pltpu.VMEM((SUBLANES, dk), F32),
                        pltpu.VMEM((SUBLANES, LANES), F32)],
        compiler_params=_cparams("parallel", "parallel", "arbitrary"),
        name="mlstm_core",
    )(proj, proj, proj, proj, gates_t, gates_t, bg, bg, w_norm.reshape(1, n_heads * dv).astype(F32))


def mlstm_mixer(hn, bsz, w_in, b_gates, w_norm, w_out, x_res):
    n_heads = b_gates.shape[0] // 2
    n_main = w_in.shape[1] - 2 * n_heads
    w_g = jnp.pad(w_in[:, n_main:], ((0, 0), (0, LANES - 2 * n_heads))).astype(BF16)
    proj = matmul(hn, w_in.astype(BF16), out_dtype=BF16, tn=1024, n_out=n_main)
    gates_t = matmul(hn, w_g, tn=LANES)[:, :2 * n_heads].T[:, None, :]
    hh = mlstm_core(proj, gates_t, bsz, b_gates, w_norm)
    return matmul(hh, w_out.astype(BF16), residual=x_res)


def _diff_attn_kernel(q_ref, k_ref, v_ref, sl_ref, lq1_ref, lk1_ref, lq2_ref, lk2_ref, g_ref, o_ref,
                      m_ref, acc_ref, *, tq, tk, lambda_init):
    qi = pl.program_id(2)
    dh = DIFF_DH
    q = (q_ref[...].astype(F32) * (dh ** -0.5 * LOG2E)).astype(BF16)
    slope_row = _lane_tile(sl_ref[0], tk // LANES)
    col = lax.broadcasted_iota(jnp.int32, (1, tk), 1).astype(F32)
    m_ref[...] = jnp.full(m_ref.shape, NEG_BIG, F32)
    acc_ref[...] = jnp.zeros(acc_ref.shape, F32)

    def step(kk, masked):
        start = pl.multiple_of(kk * tk, tk)
        cb = (col + (kk * tk - qi * tq).astype(F32)) * slope_row
        kblk = k_ref[pl.ds(start, tk), :]
        vblk = jnp.concatenate([v_ref[pl.ds(start, tk), :], jnp.ones((tk, LANES), BF16)], axis=1)
        for c in range(2):
            s = lax.dot_general(q[:, c * dh:(c + 1) * dh], kblk[:, c * dh:(c + 1) * dh],
                                (((1,), (1,)), ((), ())), preferred_element_type=F32)
            s = s + cb
            if masked:
                row_i = lax.broadcasted_iota(jnp.int32, (tq, tk), 0)
                col_i = lax.broadcasted_iota(jnp.int32, (tq, tk), 1)
                s = jnp.where(col_i <= row_i, s, NEG_BIG)
            m_prev = m_ref[c]
            m_new = jnp.maximum(m_prev, jnp.max(s, axis=1, keepdims=True))
            alpha = jnp.exp2(m_prev - m_new)
            p = jnp.exp2(s - _lane_tile(m_new, tk // LANES))
            m_ref[c] = m_new
            acc_ref[c] = (acc_ref[c] * _lane_tile(alpha, 2 * dh // LANES + 1)
                          + jnp.dot(p.astype(BF16), vblk, preferred_element_type=F32))

    def body(kk, carry):
        step(kk, False)
        return carry

    lax.fori_loop(0, qi, body, 0)
    step(qi, True)

    lam = (jnp.exp(jnp.sum(lq1_ref[...] * lk1_ref[...])) - jnp.exp(jnp.sum(lq2_ref[...] * lk2_ref[...]))
           + lambda_init)
    rep = 2 * dh // LANES
    o = (acc_ref[0, :, :2 * dh] / _lane_tile(acc_ref[0, :, 2 * dh:], rep)
         - lam * (acc_ref[1, :, :2 * dh] / _lane_tile(acc_ref[1, :, 2 * dh:], rep)))
    o = o * lax.rsqrt(jnp.mean(o * o, axis=-1, keepdims=True) + NORM_EPS) * g_ref[...]
    o_ref[...] = (o * (1.0 - lambda_init)).astype(o_ref.dtype)


def diff_attn_core(proj, bsz, lam_q1, lam_k1, lam_q2, lam_k2, w_subln, lambda_init, tq=1024):
    m, d3 = proj.shape
    d = d3 // 3
    s = m // bsz
    hw = 2 * DIFF_DH
    n_heads = d // hw
    tq = min(tq, s)
    assert s % tq == 0
    nq = s // tq
    slopes = jnp.exp2(-8.0 * jnp.arange(1, n_heads + 1, dtype=F32) / n_heads) * LOG2E
    slopes = jnp.broadcast_to(slopes[:, None, None], (n_heads, 1, LANES))
    vec = lambda a: a.reshape(1, -1).astype(F32)
    small = pl.BlockSpec((1, DIFF_DH), lambda b, h, i: (0, 0))
    return pl.pallas_call(
        functools.partial(_diff_attn_kernel, tq=tq, tk=tq, lambda_init=lambda_init),
        grid=(bsz, n_heads, nq),
        in_specs=[pl.BlockSpec((tq, hw), lambda b, h, i: (b * nq + i, h)),
                  pl.BlockSpec((s, hw), lambda b, h, i: (b, n_heads + h)),
                  pl.BlockSpec((s, hw), lambda b, h, i: (b, 2 * n_heads + h)),
                  pl.BlockSpec((1, 1, LANES), lambda b, h, i: (h, 0, 0)),
                  small, small, small, small,
                  pl.BlockSpec((1, hw), lambda b, h, i: (0, 0))],
        out_specs=pl.BlockSpec((tq, hw), lambda b, h, i: (b * nq + i, h)),
        out_shape=jax.ShapeDtypeStruct((m, d), BF16),
        scratch_shapes=[pltpu.VMEM((2, tq, LANES), F32), pltpu.VMEM((2, tq, hw + LANES), F32)],
        compiler_params=_cparams("parallel", "parallel", "arbitrary"),
        name="diff_attn",
    )(proj, proj, proj, slopes, vec(lam_q1), vec(lam_k1), vec(lam_q2), vec(lam_k2), vec(w_subln))


def diff_attn_mixer(hn, bsz, w_in, lam_q1, lam_k1, lam_q2, lam_k2, w_subln, w_out, lambda_init, x_res):
    proj = matmul(hn, w_in.astype(BF16), out_dtype=BF16, tn=1024)
    o = diff_attn_core(proj, bsz, lam_q1, lam_k1, lam_q2, lam_k2, w_subln, lambda_init)
    return matmul(o, w_out.astype(BF16), residual=x_res)


def _mm_conv_silu_kernel(x_ref, w_ref, cw_ref, cb_ref, o_ref, tail_ref, *, width, tiles_per_seq):
    i = pl.program_id(0)
    j = pl.program_id(1)

    @pl.when((i == 0) & (j == 0))
    def _():
        tail_ref[...] = jnp.zeros_like(tail_ref)

    u = jnp.dot(x_ref[...], w_ref[...], preferred_element_type=F32)
    tail = jnp.where(i % tiles_per_seq == 0, 0.0, tail_ref[j])
    y = cw_ref[width - 1:width, :] * u + cb_ref[...]
    for t in range(width - 1):
        y = y + cw_ref[t:t + 1, :] * _shift_rows(u, tail, width - 1 - t)
    tail_ref[j] = u[-SUBLANES:]
    o_ref[...] = (y * jax.nn.sigmoid(y)).astype(o_ref.dtype)


def matmul_conv_silu(x, w, col0, conv_w, conv_b, bsz, tm=1024, tn=512):
    m, k = x.shape
    width, ch = conv_w.shape
    s = m // bsz
    tm = _pick_tile(s, tm, SUBLANES)
    tn = _pick_tile(math.gcd(ch, col0) if col0 else ch, tn, LANES)
    jb0 = col0 // tn
    return pl.pallas_call(
        functools.partial(_mm_conv_silu_kernel, width=width, tiles_per_seq=s // tm),
        grid=(m // tm, ch // tn),
        in_specs=[pl.BlockSpec((tm, k), lambda i, j: (i, 0)),
                  pl.BlockSpec((k, tn), lambda i, j: (0, jb0 + j)),
                  pl.BlockSpec((width, tn), lambda i, j: (0, j)),
                  pl.BlockSpec((1, tn), lambda i, j: (0, j))],
        out_specs=pl.BlockSpec((tm, tn), lambda i, j: (i, j)),
        out_shape=jax.ShapeDtypeStruct((m, ch), BF16),
        scratch_shapes=[pltpu.VMEM((ch // tn, SUBLANES, tn), F32)],
        compiler_params=_cparams("arbitrary", "arbitrary"),
        name="matmul_conv_silu",
    )(x, w, conv_w.astype(F32), conv_b.reshape(1, ch).astype(F32))


def _split3(v):
    hi = v.astype(BF16)
    r = v - hi.astype(F32)
    mid = r.astype(BF16)
    lo = (r - mid.astype(F32)).astype(BF16)
    return hi, mid, lo


def _dot_exact_rhs(lhs_f32, rhs_bf16):
    n = lhs_f32.shape[0]
    out = jnp.dot(jnp.concatenate(_split3(lhs_f32), axis=0), rhs_bf16, preferred_element_type=F32)
    return out[:n] + out[n:2 * n] + out[2 * n:]


def _dot_exact_lhs(lhs_bf16, rhs_f32):
    hi, mid, lo = _split3(rhs_f32)
    return (jnp.dot(lhs_bf16, hi, preferred_element_type=F32) + jnp.dot(lhs_bf16, mid, preferred_element_type=F32)
            + jnp.dot(lhs_bf16, lo, preferred_element_type=F32))


def _softplus(x):
    return jnp.maximum(x, 0.0) + jnp.log1p(jnp.exp(-jnp.abs(x)))


def _ssd_kernel(x_ref, b_ref, c_ref, z_ref, dtc_ref, dtp_ref, dtbc_ref, alc_ref, dtbp_ref, alp_ref, dsk_ref,
                wn_ref, ep_ref, el_ref, o_ref, st_ref, *, L, n_chunks, hpg, p):
    @pl.when(pl.program_id(2) == 0)
    def _():
        st_ref[...] = jnp.zeros_like(st_ref)

    npairs = hpg // 2
    a_c = -jnp.exp(alc_ref[0])
    a_p = -jnp.exp(alp_ref[0])
    tri = (lax.broadcasted_iota(jnp.int32, (L, L), 1) <= lax.broadcasted_iota(jnp.int32, (L, L), 0)).astype(BF16)
    r2 = lax.broadcasted_iota(jnp.int32, (2 * L, 2 * L), 0)
    c2 = lax.broadcasted_iota(jnp.int32, (2 * L, 2 * L), 1)
    bd = ((r2 <= c2) & ((r2 < L) == (c2 < L))).astype(BF16)
    lane2 = lax.broadcasted_iota(jnp.int32, (L, 2 * L), 1)
    row2 = lax.broadcasted_iota(jnp.int32, (L, 2 * L), 0)
    tri2 = jnp.where(lane2 < L, lane2, lane2 - L) <= row2
    lane_p = lax.broadcasted_iota(jnp.int32, (L, 2 * p), 1)

    def chunk(c, carry):
        r0 = pl.multiple_of(c * L, L)
        x = x_ref[pl.ds(r0, L), :].astype(F32)
        bmat = b_ref[pl.ds(r0, L), :]
        cmat = c_ref[pl.ds(r0, L), :]
        dt_c = _softplus(dtc_ref[0, pl.ds(r0, L), :] + dtbc_ref[0])
        cum_c = _dot_exact_lhs(tri, dt_c * a_c)
        p0 = pl.multiple_of(c * npairs, npairs)
        dt_p = _softplus(dtp_ref[0, pl.ds(p0, npairs), :] + dtbp_ref[0])
        cum_p = _dot_exact_rhs(dt_p * a_p, bd)
        both = _dot_exact_rhs(jnp.concatenate([cum_c, dt_c], axis=0), ep_ref[...])
        cum_e, dt_e = both[:L], both[L:]
        cum_l = cum_e if el_ref is None else _dot_exact_rhs(cum_c, el_ref[...])
        cb = lax.dot_general(cmat, bmat, (((1,), (1,)), ((), ())), preferred_element_type=F32)
        cb2 = jnp.concatenate([cb, cb], axis=1)
        x16 = x.astype(BF16)
        ys = []
        for j in range(npairs):
            seg = cum_l[:, j * 2 * L:(j + 1) * 2 * L] - cum_p[j:j + 1, :]
            w = jnp.exp(jnp.where(tri2, seg, NEG_BIG)) * cb2 * dt_p[j:j + 1, :]
            xp = x16[:, j * 2 * p:(j + 1) * 2 * p]
            rhs = jnp.concatenate([jnp.where(lane_p < p, xp, 0), jnp.where(lane_p >= p, xp, 0)], axis=0)
            ys.append(jnp.dot(w.astype(BF16), rhs, preferred_element_type=F32))
        y = jnp.concatenate(ys, axis=1)
        st = st_ref[...]
        y = y + jnp.dot(cmat, st.astype(BF16), preferred_element_type=F32) * jnp.exp(cum_e)
        cum_last = cum_e[L - 1:L, :]
        xw = (x * (jnp.exp(cum_last - cum_e) * dt_e)).astype(BF16)
        st_ref[...] = jnp.exp(cum_last) * st + jnp.dot(bmat.astype(F32).T.astype(BF16), xw,
                                                       preferred_element_type=F32)
        y = y + x * dsk_ref[0]
        zf = z_ref[pl.ds(r0, L), :].astype(F32)
        y = y * (zf * jax.nn.sigmoid(zf))
        y = y * lax.rsqrt(jnp.mean(y * y, axis=1, keepdims=True) + NORM_EPS) * wn_ref[0]
        o_ref[pl.ds(r0, L), :] = y.astype(o_ref.dtype)
        return carry

    lax.fori_loop(0, n_chunks, chunk, 0, unroll=8)


def _kernel_no_el(*refs, **kw):
    *ins, o_ref, st_ref = refs
    _ssd_kernel(*ins, None, o_ref, st_ref, **kw)


def ssd_core(proj, xbc, dt_raw, bsz, dt_bias, a_log, d_skip, w_norm, t_rows=512):
    m = proj.shape[0]
    s = m // bsz
    nh = dt_bias.shape[0]
    p, n, g, L = SSM_HEADDIM, SSM_STATE, SSM_GROUPS, SSM_CHUNK
    di = nh * p
    hpg = nh // g
    gw = hpg * p
    t_rows = min(t_rows, s)
    assert s % t_rows == 0 and t_rows % L == 0 and hpg % 2 == 0 and 2 * L == LANES
    nt = s // t_rows
    n_chunks = t_rows // L
    npairs = hpg // 2
    nc_all = m // L
    dt_c = dt_raw.reshape(m, g, hpg).transpose(1, 0, 2)
    dt_p = (dt_raw.reshape(nc_all, L, g, npairs, 2).transpose(2, 0, 3, 4, 1)
            .reshape(g, nc_all * npairs, 2 * L))
    col = lambda a: a.astype(F32).reshape(g, 1, hpg)
    pair = lambda a: jnp.broadcast_to(a.astype(F32).reshape(g, npairs, 2, 1), (g, npairs, 2, L)).reshape(g, npairs, 2 * L)
    expand = lambda rep: jnp.repeat(jnp.eye(hpg, dtype=BF16), rep, axis=1)
    d_e = jnp.repeat(d_skip.astype(F32), p).reshape(g, 1, gw)
    same = p == L
    row = lambda b, gg, t: b * nt + t
    gspec = lambda shape: pl.BlockSpec((1,) + shape, lambda b, gg, t: (gg, 0, 0))
    in_specs = [pl.BlockSpec((t_rows, gw), lambda b, gg, t: (row(b, gg, t), gg)),
                pl.BlockSpec((t_rows, n), lambda b, gg, t: (row(b, gg, t), di // n + gg)),
                pl.BlockSpec((t_rows, n), lambda b, gg, t: (row(b, gg, t), di // n + g + gg)),
                pl.BlockSpec((t_rows, gw), lambda b, gg, t: (row(b, gg, t), gg)),
                pl.BlockSpec((1, t_rows, hpg), lambda b, gg, t: (gg, row(b, gg, t), 0)),
                pl.BlockSpec((1, n_chunks * npairs, 2 * L), lambda b, gg, t: (gg, row(b, gg, t), 0)),
                gspec((1, hpg)), gspec((1, hpg)), gspec((npairs, 2 * L)), gspec((npairs, 2 * L)),
                gspec((1, gw)), gspec((1, gw)),
                pl.BlockSpec((hpg, gw), lambda b, gg, t: (0, 0))]
    args = [xbc, xbc, xbc, proj, dt_c, dt_p, col(dt_bias), col(a_log), pair(dt_bias), pair(a_log), d_e,
            w_norm.astype(F32).reshape(g, 1, gw), expand(p)]
    if not same:
        in_specs.append(pl.BlockSpec((hpg, hpg * L), lambda b, gg, t: (0, 0)))
        args.append(expand(L))
    kern = functools.partial(_ssd_kernel if not same else _kernel_no_el, L=L, n_chunks=n_chunks, hpg=hpg, p=p)
    return pl.pallas_call(
        kern,
        grid=(bsz, g, nt),
        in_specs=in_specs,
        out_specs=pl.BlockSpec((t_rows, gw), lambda b, gg, t: (row(b, gg, t), gg)),
        out_shape=jax.ShapeDtypeStruct((m, di), BF16),
        scratch_shapes=[pltpu.VMEM((n, gw), F32)],
        compiler_params=_cparams("parallel", "parallel", "arbitrary"),
        name="ssd_core",
    )(*args)


def mamba2_mixer(hn, bsz, w_in, conv_w, conv_b, dt_bias, a_log, d_skip, w_norm, w_out, x_res):
    nh = dt_bias.shape[0]
    di = nh * SSM_HEADDIM
    n_main = w_in.shape[1] - nh
    w_bf = w_in.astype(BF16)
    z = matmul(hn, w_bf, out_dtype=BF16, tn=1024, n_out=di)
    xbc = matmul_conv_silu(hn, w_bf, di, conv_w, conv_b, bsz)
    dt_raw = matmul(hn, w_bf, tn=LANES, col0=n_main, n_out=nh)
    y = ssd_core(z, xbc, dt_raw, bsz, dt_bias, a_log, d_skip, w_norm)
    return matmul(y, w_out.astype(BF16), residual=x_res, tm=512, tk=w_out.shape[0])


def _hgrn2_kernel(q_ref, f_ref, i_ref, g_ref, lb_ref, wn_ref, o_ref, st_ref, *, L, sub, n_chunks, layer):
    @pl.when(pl.program_id(2) == 0)
    def _():
        st_ref[...] = jnp.zeros_like(st_ref)

    dk = q_ref.shape[1]
    lbp = lb_ref[...]
    e = jnp.exp(lbp - jnp.max(lbp, axis=0, keepdims=True))
    sm = e / jnp.sum(e, axis=0, keepdims=True)
    lb = jnp.sum(sm[1:layer + 1], axis=0, keepdims=True) if layer > 0 else jnp.zeros((1, dk), F32)
    log_lb = jnp.log(lb)
    log_1mlb = jnp.log1p(-lb)
    tri = (lax.broadcasted_iota(jnp.int32, (L, L), 1) <= lax.broadcasted_iota(jnp.int32, (L, L), 0)).astype(BF16)
    sub_row = lax.broadcasted_iota(jnp.int32, (sub, dk), 0)
    lane_s = lax.broadcasted_iota(jnp.int32, (sub, sub), 1)
    nsub = L // sub

    def chunk(c, carry):
        r0 = pl.multiple_of(c * L, L)
        qf = q_ref[pl.ds(r0, L), :].astype(F32)
        qa = qf * jax.nn.sigmoid(qf)
        fx = f_ref[pl.ds(r0, L), :].astype(F32)
        key = (1.0 - lb) * jax.nn.sigmoid(-fx)
        log_f = jnp.logaddexp(log_lb, log_1mlb + jax.nn.log_sigmoid(fx))
        b = _dot_exact_lhs(tri, log_f)
        v = i_ref[pl.ds(r0, L), :]
        strips = []
        for blk in range(nsub):
            t0 = blk * sub
            bq = b[t0:t0 + sub]
            qb = qa[t0:t0 + sub]
            cols = []
            if blk > 0:
                ref_row = b[t0 - 1:t0]
                qd = (qb * jnp.exp(bq - ref_row)).astype(BF16)
                kd = (key[:t0] * jnp.exp(ref_row - b[:t0])).astype(BF16)
                cols.append(lax.dot_general(qd, kd, (((1,), (1,)), ((), ())), preferred_element_type=F32))
            diag = jnp.zeros((sub, sub), F32)
            for sl in range(sub):
                srow = t0 + sl
                dec = jnp.exp(jnp.where(sub_row >= sl, bq - b[srow:srow + 1], NEG_BIG))
                col = jnp.sum(qb * dec * key[srow:srow + 1], axis=1, keepdims=True)
                diag = jnp.where(lane_s == sl, col, diag)
            cols.append(diag)
            if t0 + sub < L:
                cols.append(jnp.zeros((sub, L - t0 - sub), F32))
            strips.append(jnp.concatenate(cols, axis=1))
        attn = jnp.concatenate(strips, axis=0)
        st = st_ref[...]
        o = (jnp.dot(attn.astype(BF16), v, preferred_element_type=F32)
             + lax.dot_general((qa * jnp.exp(b)).astype(BF16), st.astype(BF16), (((1,), (1,)), ((), ())),
                               preferred_element_type=F32))
        b_last = b[L - 1:L]
        kd = (key * jnp.exp(b_last - b)).astype(BF16)
        st_ref[...] = jnp.exp(b_last) * st + jnp.dot(v.astype(F32).T.astype(BF16), kd, preferred_element_type=F32)
        o = o * lax.rsqrt(jnp.mean(o * o, axis=1, keepdims=True) + NORM_EPS) * wn_ref[...]
        gf = g_ref[pl.ds(r0, L), :].astype(F32)
        o_ref[pl.ds(r0, L), :] = (o * (gf * jax.nn.sigmoid(gf))).astype(o_ref.dtype)
        return carry

    lax.fori_loop(0, n_chunks, chunk, 0, unroll=16)


def hgrn2_core(proj, bsz, lb_params, layer, w_gnorm, t_rows=1024):
    m, d4 = proj.shape
    d = d4 // 4
    s = m // bsz
    dk, L = HGRN_DK, HGRN_CHUNK
    n_heads = d // dk
    assert d // n_heads == dk, "kernel assumes DV == DK"
    t_rows = min(t_rows, s)
    assert s % t_rows == 0 and t_rows % L == 0
    nt = s // t_rows
    depth = lb_params.shape[0]
    sec = lambda k: pl.BlockSpec((t_rows, dk), lambda b, h, t: (b * nt + t, k * n_heads + h))
    return pl.pallas_call(
        functools.partial(_hgrn2_kernel, L=L, sub=SUBLANES, n_chunks=t_rows // L, layer=layer),
        grid=(bsz, n_heads, nt),
        in_specs=[sec(0), sec(1), sec(2), sec(3),
                  pl.BlockSpec((depth, dk), lambda b, h, t: (0, h)),
                  pl.BlockSpec((1, dk), lambda b, h, t: (0, 0))],
        out_specs=pl.BlockSpec((t_rows, dk), lambda b, h, t: (b * nt + t, h)),
        out_shape=jax.ShapeDtypeStruct((m, d), BF16),
        scratch_shapes=[pltpu.VMEM((dk, dk), F32)],
        compiler_params=_cparams("parallel", "parallel", "arbitrary"),
        name="hgrn2_core",
    )(proj, proj, proj, proj, lb_params.astype(F32), w_gnorm.reshape(1, dk).astype(F32))


def hgrn2_mixer(hn, bsz, w_in, lb_params, layer, w_gnorm, w_out, x_res):
    proj = matmul(hn, w_in.astype(BF16), out_dtype=BF16, tn=1024)
    o = hgrn2_core(proj, bsz, lb_params, layer, w_gnorm)
    return matmul(o, w_out.astype(BF16), residual=x_res)


def conv_ffn(x2d, bsz, g_norm, w_up, conv_w, conv_b, w_down):
    hn = rmsnorm(x2d, g_norm)
    a = ffn_up_conv_act(hn, w_up.astype(BF16), conv_w, conv_b, bsz)
    return matmul(a, w_down.astype(BF16), residual=x2d, tm=512, tk=w_down.shape[0])


def kernel(x, norm_mix, norm_ffn, norm_final, mlstm_w_in, mlstm_b_gates, mlstm_w_norm, mlstm_w_out, diff_w_in, diff_lam_q1, diff_lam_k1, diff_lam_q2, diff_lam_k2, diff_w_subln, diff_w_out, ssm_w_in, ssm_conv_w, ssm_conv_b, ssm_dt_bias, ssm_a_log, ssm_d, ssm_w_norm, ssm_w_out, hgrn_w_in, hgrn_lb, hgrn_w_gnorm, hgrn_w_out, ffn_w_up, ffn_conv_w, ffn_conv_b, ffn_w_down):
    bsz, s, d = x.shape
    depth = norm_mix.shape[0]
    x2d = x.reshape(bsz * s, d)
    for layer in range(depth):
        kind, j = layer % N_MIXERS, layer // N_MIXERS
        hn = rmsnorm(x2d, norm_mix[layer])
        if kind == 0:
            x2d = mlstm_mixer(hn, bsz, mlstm_w_in[j], mlstm_b_gates[j], mlstm_w_norm[j], mlstm_w_out[j], x2d)
        elif kind == 1:
            lambda_init = 0.8 - 0.6 * math.exp(-0.3 * layer)
            x2d = diff_attn_mixer(hn, bsz, diff_w_in[j], diff_lam_q1[j], diff_lam_k1[j], diff_lam_q2[j],
                                  diff_lam_k2[j], diff_w_subln[j], diff_w_out[j], lambda_init, x2d)
        elif kind == 2:
            x2d = mamba2_mixer(hn, bsz, ssm_w_in[j], ssm_conv_w[j], ssm_conv_b[j], ssm_dt_bias[j],
                               ssm_a_log[j], ssm_d[j], ssm_w_norm[j], ssm_w_out[j], x2d)
        else:
            x2d = hgrn2_mixer(hn, bsz, hgrn_w_in[j], hgrn_lb, layer, hgrn_w_gnorm[j], hgrn_w_out[j], x2d)
        x2d = conv_ffn(x2d, bsz, norm_ffn[layer], ffn_w_up[layer], ffn_conv_w[layer], ffn_conv_b[layer],
                       ffn_w_down[layer])
    return rmsnorm(x2d, norm_final, out_dtype=x.dtype).reshape(bsz, s, d)
```

```python
import functools
import math

import jax
import jax.numpy as jnp
from jax import lax
from jax.experimental import pallas as pl
from jax.experimental.pallas import tpu as pltpu

NORM_EPS = 1e-6
N_MIXERS = 4
DIFF_DH = 128
SSM_HEADDIM = 64
SSM_STATE = 128
SSM_GROUPS = 8
SSM_CHUNK = 64
HGRN_DK = 128
HGRN_CHUNK = 64
LOG2E = 1.4426950408889634
NEG_BIG = -1e30

V7X_VMEM_BYTES = 64 * 1024 * 1024
VMEM_LIMIT_BYTES = V7X_VMEM_BYTES - 8 * 1024 * 1024
LANES = 128
SUBLANES = 8

F32 = jnp.float32
BF16 = jnp.bfloat16


def _cparams(*sem):
    return pltpu.CompilerParams(dimension_semantics=sem, vmem_limit_bytes=VMEM_LIMIT_BYTES)


def _lane_tile(x, n):
    return x if n == 1 else jnp.concatenate([x] * n, axis=1)


def _mm_kernel(x_ref, w_ref, *rest, has_res, nk):
    if has_res:
        r_ref, o_ref = rest
    else:
        (o_ref,) = rest
    part = jnp.dot(x_ref[...], w_ref[...], preferred_element_type=F32)
    if nk == 1:
        if has_res:
            part = part + r_ref[...]
        o_ref[...] = part.astype(o_ref.dtype)
    else:
        k = pl.program_id(2)

        @pl.when(k == 0)
        def _():
            if has_res:
                o_ref[...] = r_ref[...] + part
            else:
                o_ref[...] = part

        @pl.when(k > 0)
        def _():
            o_ref[...] += part


def _pick_tile(n, target, quantum):
    best = None
    t = quantum
    while t <= min(n, target):
        if n % t == 0:
            best = t
        t += quantum
    assert best is not None, (n, target, quantum)
    return best


def matmul(x, w, residual=None, out_dtype=F32, tm=1024, tn=512, tk=4096, col0=0, n_out=None):
    m, k = x.shape
    k2 = w.shape[0]
    n = w.shape[1] - col0 if n_out is None else n_out
    assert k == k2
    tm = _pick_tile(m, tm, SUBLANES)
    tn = _pick_tile(math.gcd(n, col0) if col0 else n, tn, LANES)
    tk = k if k <= tk else _pick_tile(k, max(tk, 5504), LANES)
    nk = k // tk
    if nk > 1:
        assert out_dtype == F32
    has_res = residual is not None
    jb0 = col0 // tn
    in_specs = [pl.BlockSpec((tm, tk), lambda i, j, kk: (i, kk)),
                pl.BlockSpec((tk, tn), lambda i, j, kk: (kk, jb0 + j))]
    args = [x, w]
    if has_res:
        in_specs.append(pl.BlockSpec((tm, tn), lambda i, j, kk: (i, j)))
        args.append(residual)
    return pl.pallas_call(
        functools.partial(_mm_kernel, has_res=has_res, nk=nk),
        grid=(m // tm, n // tn, nk),
        in_specs=in_specs,
        out_specs=pl.BlockSpec((tm, tn), lambda i, j, kk: (i, j)),
        out_shape=jax.ShapeDtypeStruct((m, n), out_dtype),
        compiler_params=_cparams("parallel", "parallel", "arbitrary"),
        name="matmul",
    )(*args)


def _rmsnorm_kernel(x_ref, g_ref, o_ref):
    x = x_ref[...].astype(F32)
    ms = jnp.mean(x * x, axis=-1, keepdims=True)
    o_ref[...] = (x * lax.rsqrt(ms + NORM_EPS) * g_ref[...]).astype(o_ref.dtype)


def rmsnorm(x, g, out_dtype=BF16, tm=512):
    m, d = x.shape
    tm = _pick_tile(m, tm, SUBLANES)
    return pl.pallas_call(
        _rmsnorm_kernel,
        grid=(m // tm,),
        in_specs=[pl.BlockSpec((tm, d), lambda i: (i, 0)),
                  pl.BlockSpec((1, d), lambda i: (0, 0))],
        out_specs=pl.BlockSpec((tm, d), lambda i: (i, 0)),
        out_shape=jax.ShapeDtypeStruct((m, d), out_dtype),
        compiler_params=_cparams("parallel"),
        name="rmsnorm",
    )(x, g.reshape(1, d).astype(F32))


def _shift_rows(cur, prev_tail, shift):
    rolled = pltpu.roll(cur, shift, axis=0)
    tail = pltpu.roll(prev_tail, shift, axis=0)
    row = lax.broadcasted_iota(jnp.int32, (SUBLANES, cur.shape[1]), 0)
    head = jnp.where(row < shift, tail, rolled[:SUBLANES])
    return jnp.concatenate([head, rolled[SUBLANES:]], axis=0)


def _ffn_up_kernel(x_ref, wg_ref, wv_ref, cg_ref, cv_ref, bg_ref, bv_ref, o_ref,
                   ug_ref, uv_ref, tg_ref, tv_ref, *, tiles_per_seq, ncb):
    t = pl.program_id(0)

    @pl.when(t == 0)
    def _():
        for r in (ug_ref, uv_ref, tg_ref, tv_ref):
            r[...] = jnp.zeros_like(r)

    x = x_ref[...]
    ug_new = jnp.dot(x, wg_ref[...], preferred_element_type=F32)
    uv_new = jnp.dot(x, wv_ref[...], preferred_element_type=F32)
    prev = jnp.maximum(t - 1, 0)
    pj = prev % ncb
    seq_start = (prev // ncb) % tiles_per_seq == 0

    def conv(u, t_ref, w_ref, b_ref):
        old = t_ref[pj]
        tail = jnp.where(seq_start, 0.0, old)
        y = (w_ref[2:3, :] * u + w_ref[1:2, :] * _shift_rows(u, tail, 1)
             + w_ref[0:1, :] * _shift_rows(u, tail, 2) + b_ref[...])
        t_ref[pj] = jnp.where(t > 0, u[-SUBLANES:], old)
        return y

    g = conv(ug_ref[...], tg_ref, cg_ref, bg_ref)
    v = conv(uv_ref[...], tv_ref, cv_ref, bv_ref)
    o_ref[...] = (g * jax.nn.sigmoid(g) * v).astype(o_ref.dtype)
    ug_ref[...] = ug_new
    uv_ref[...] = uv_new


def ffn_up_conv_act(hn, w_up, conv_w, conv_b, bsz, tm=1024, tn=256):
    m, k = hn.shape
    f = w_up.shape[1] // 2
    s = m // bsz
    tm = _pick_tile(s, tm, SUBLANES)
    tn = _pick_tile(f, tn, LANES)
    ncb = f // tn
    n_tiles = (m // tm) * ncb
    width = conv_w.shape[0]
    assert width == 3
    cw = conv_w.astype(F32)
    cb = conv_b.reshape(1, 2 * f).astype(F32)
    cur = lambda t: jnp.minimum(t, n_tiles - 1)
    prev = lambda t: jnp.maximum(t - 1, 0)
    return pl.pallas_call(
        functools.partial(_ffn_up_kernel, tiles_per_seq=s // tm, ncb=ncb),
        grid=(n_tiles + 1,),
        in_specs=[pl.BlockSpec((tm, k), lambda t: (cur(t) // ncb, 0)),
                  pl.BlockSpec((k, tn), lambda t: (0, cur(t) % ncb)),
                  pl.BlockSpec((k, tn), lambda t: (0, cur(t) % ncb + ncb)),
                  pl.BlockSpec((width, tn), lambda t: (0, prev(t) % ncb)),
                  pl.BlockSpec((width, tn), lambda t: (0, prev(t) % ncb + ncb)),
                  pl.BlockSpec((1, tn), lambda t: (0, prev(t) % ncb)),
                  pl.BlockSpec((1, tn), lambda t: (0, prev(t) % ncb + ncb))],
        out_specs=pl.BlockSpec((tm, tn), lambda t: (prev(t) // ncb, prev(t) % ncb)),
        out_shape=jax.ShapeDtypeStruct((m, f), BF16),
        scratch_shapes=[pltpu.VMEM((tm, tn), F32), pltpu.VMEM((tm, tn), F32),
                        pltpu.VMEM((ncb, SUBLANES, tn), F32), pltpu.VMEM((ncb, SUBLANES, tn), F32)],
        compiler_params=_cparams("arbitrary"),
        name="ffn_up_conv_act",
    )(hn, w_up, w_up, cw, cw, cb, cb)


def _lane_cumsum(x):
    lane = lax.broadcasted_iota(jnp.int32, x.shape, 1)
    shift = 1
    while shift < x.shape[1]:
        x = x + jnp.where(lane >= shift, pltpu.roll(x, shift, axis=1), 0.0)
        shift *= 2
    return x


def _mlstm_kernel(q_ref, k_ref, v_ref, og_ref, gi_ref, gf_ref, bi_ref, bf_ref, wn_ref, out_ref,
                  c_ref, n_ref, m_ref, *, n_chunks):
    L = LANES
    dk = q_ref.shape[1]
    dv = v_ref.shape[1]

    @pl.when(pl.program_id(2) == 0)
    def _():
        c_ref[...] = jnp.zeros_like(c_ref)
        n_ref[...] = jnp.zeros_like(n_ref)
        m_ref[...] = jnp.zeros_like(m_ref)

    row_i = lax.broadcasted_iota(jnp.int32, (L, L), 0)
    col_i = lax.broadcasted_iota(jnp.int32, (L, L), 1)
    causal = col_i <= row_i
    bias_i = bi_ref[0]
    bias_f = bf_ref[0]

    for c in range(n_chunks):
        rows = slice(c * L, (c + 1) * L)
        q = (q_ref[rows, :].astype(F32) * (dk ** -0.5)).astype(BF16)
        k = k_ref[rows, :]
        v = v_ref[rows, :]
        i_row = gi_ref[0, :, rows] + bias_i
        f_raw = gf_ref[0, :, rows] + bias_f
        lf_row = jnp.minimum(f_raw, 0.0) - jnp.log1p(jnp.exp(-jnp.abs(f_raw)))
        b_row = _lane_cumsum(lf_row)
        b_last = jnp.min(b_row, axis=1, keepdims=True)
        bm = jnp.broadcast_to(b_row, (L, L))
        im = jnp.broadcast_to(i_row, (L, L))
        bt = bm.T
        it = im.T
        m_prev = m_ref[0:1, :]
        d = jnp.where(causal, bt - bm + im, NEG_BIG)
        a = bt + m_prev
        m_t = jnp.maximum(a, jnp.max(d, axis=1, keepdims=True))
        w_intra = jnp.exp(d - m_t)
        w_inter = jnp.exp(a - m_t)
        sc = lax.dot_general(q, k, (((1,), (1,)), ((), ())), preferred_element_type=F32) * w_intra
        num = (jnp.dot(sc.astype(BF16), v, preferred_element_type=F32)
               + _lane_tile(w_inter, dv // LANES)
               * jnp.dot(q, c_ref[...].astype(BF16), preferred_element_type=F32))
        qn = jnp.sum(q.astype(F32) * n_ref[0:1, :], axis=1, keepdims=True)
        den = jnp.sum(sc, axis=1, keepdims=True) + w_inter * qn
        den = jnp.maximum(jnp.abs(den), jnp.exp(-m_t))
        hc = num / _lane_tile(den, dv // LANES)
        hc = hc * lax.rsqrt(jnp.mean(hc * hc, axis=1, keepdims=True) + NORM_EPS) * wn_ref[...]
        out_ref[rows, :] = (jax.nn.sigmoid(og_ref[rows, :].astype(F32)) * hc).astype(out_ref.dtype)

        g = b_last - bt + it
        m_new = jnp.maximum(b_last + m_prev, jnp.max(g, axis=0, keepdims=True))
        ws = jnp.exp(g - m_new)
        decay = jnp.exp(b_last + m_prev - m_new)
        kw = k.astype(F32) * _lane_tile(ws, dk // LANES)
        c_ref[...] = (_lane_tile(decay, dv // LANES) * c_ref[...]
                      + jnp.dot(kw.T.astype(BF16), v, preferred_element_type=F32))
        n_ref[...] = (_lane_tile(decay, dk // LANES) * n_ref[...]
                      + jnp.sum(kw, axis=0, keepdims=True))
        m_ref[...] = jnp.broadcast_to(m_new, m_ref.shape)


def mlstm_core(proj, gates_t, bsz, b_gates, w_norm, t_rows=512):
    m = proj.shape[0]
    s = m // bsz
    n_heads = b_gates.shape[0] // 2
    dv = w_norm.shape[0] // n_heads
    dk = dv // 2
    t_rows = min(t_rows, s)
    assert s % t_rows == 0 and t_rows % LANES == 0 and dk % LANES == 0
    nt = s // t_rows
    bg = jnp.broadcast_to(b_gates.astype(F32)[:, None, None], (2 * n_heads, 1, LANES))
    row = lambda b, h, t: b * nt + t
    return pl.pallas_call(
        functools.partial(_mlstm_kernel, n_chunks=t_rows // LANES),
        grid=(bsz, n_heads, nt),
        in_specs=[pl.BlockSpec((t_rows, dk), lambda b, h, t: (row(b, h, t), h)),
                  pl.BlockSpec((t_rows, dk), lambda b, h, t: (row(b, h, t), n_heads + h)),
                  pl.BlockSpec((t_rows, dv), lambda b, h, t: (row(b, h, t), n_heads + h)),
                  pl.BlockSpec((t_rows, dv), lambda b, h, t: (row(b, h, t), 2 * n_heads + h)),
                  pl.BlockSpec((1, 1, t_rows), lambda b, h, t: (h, 0, row(b, h, t))),
                  pl.BlockSpec((1, 1, t_rows), lambda b, h, t: (n_heads + h, 0, row(b, h, t))),
                  pl.BlockSpec((1, 1, LANES), lambda b, h, t: (h, 0, 0)),
                  pl.BlockSpec((1, 1, LANES), lambda b, h, t: (n_heads + h, 0, 0)),
                  pl.BlockSpec((1, dv), lambda b, h, t: (0, h))],
        out_specs=pl.BlockSpec((t_rows, dv), lambda b, h, t: (row(b, h, t), h)),
        out_shape=jax.ShapeDtypeStruct((m, n_heads * dv), BF16),
        scratch_shapes=[pltpu.VMEM((dk, dv), F32), pltpu.VMEM((SUBLANES, dk), F32),
                        pltpu.VMEM((SUBLANES, LANES), F32)],
        compiler_params=_cparams("parallel", "parallel", "arbitrary"),
        name="mlstm_core",
    )(proj, proj, proj, proj, gates_t, gates_t, bg, bg, w_norm.reshape(1, n_heads * dv).astype(F32))


def mlstm_mixer(hn, bsz, w_in, b_gates, w_norm, w_out, x_res):
    n_heads = b_gates.shape[0] // 2
    n_main = w_in.shape[1] - 2 * n_heads
    w_g = jnp.pad(w_in[:, n_main:], ((0, 0), (0, LANES - 2 * n_heads))).astype(BF16)
    proj = matmul(hn, w_in.astype(BF16), out_dtype=BF16, tn=1024, n_out=n_main)
    gates_t = matmul(hn, w_g, tn=LANES)[:, :2 * n_heads].T[:, None, :]
    hh = mlstm_core(proj, gates_t, bsz, b_gates, w_norm)
    return matmul(hh, w_out.astype(BF16), residual=x_res)


def _diff_attn_kernel(q_ref, k_ref, v_ref, sl_ref, lq1_ref, lk1_ref, lq2_ref, lk2_ref, g_ref, o_ref,
                      m_ref, acc_ref, *, tq, tk, lambda_init):
    qi = pl.program_id(2)
    dh = DIFF_DH
    q = (q_ref[...].astype(F32) * (dh ** -0.5 * LOG2E)).astype(BF16)
    slope_row = _lane_tile(sl_ref[0], tk // LANES)
    col = lax.broadcasted_iota(jnp.int32, (1, tk), 1).astype(F32)
    m_ref[...] = jnp.full(m_ref.shape, NEG_BIG, F32)
    acc_ref[...] = jnp.zeros(acc_ref.shape, F32)

    def step(kk, masked):
        start = pl.multiple_of(kk * tk, tk)
        cb = (col + (kk * tk - qi * tq).astype(F32)) * slope_row
        kblk = k_ref[pl.ds(start, tk), :]
        vblk = jnp.concatenate([v_ref[pl.ds(start, tk), :], jnp.ones((tk, LANES), BF16)], axis=1)
        for c in range(2):
            s = lax.dot_general(q[:, c * dh:(c + 1) * dh], kblk[:, c * dh:(c + 1) * dh],
                                (((1,), (1,)), ((), ())), preferred_element_type=F32)
            s = s + cb
            if masked:
                row_i = lax.broadcasted_iota(jnp.int32, (tq, tk), 0)
                col_i = lax.broadcasted_iota(jnp.int32, (tq, tk), 1)
                s = jnp.where(col_i <= row_i, s, NEG_BIG)
            m_prev = m_ref[c]
            m_new = jnp.maximum(m_prev, jnp.max(s, axis=1, keepdims=True))
            alpha = jnp.exp2(m_prev - m_new)
            p = jnp.exp2(s - _lane_tile(m_new, tk // LANES))
            m_ref[c] = m_new
            acc_ref[c] = (acc_ref[c] * _lane_tile(alpha, 2 * dh // LANES + 1)
                          + jnp.dot(p.astype(BF16), vblk, preferred_element_type=F32))

    def body(kk, carry):
        step(kk, False)
        return carry

    lax.fori_loop(0, qi, body, 0)
    step(qi, True)

    lam = (jnp.exp(jnp.sum(lq1_ref[...] * lk1_ref[...])) - jnp.exp(jnp.sum(lq2_ref[...] * lk2_ref[...]))
           + lambda_init)
    rep = 2 * dh // LANES
    o = (acc_ref[0, :, :2 * dh] / _lane_tile(acc_ref[0, :, 2 * dh:], rep)
         - lam * (acc_ref[1, :, :2 * dh] / _lane_tile(acc_ref[1, :, 2 * dh:], rep)))
    o = o * lax.rsqrt(jnp.mean(o * o, axis=-1, keepdims=True) + NORM_EPS) * g_ref[...]
    o_ref[...] = (o * (1.0 - lambda_init)).astype(o_ref.dtype)


def diff_attn_core(proj, bsz, lam_q1, lam_k1, lam_q2, lam_k2, w_subln, lambda_init, tq=1024):
    m, d3 = proj.shape
    d = d3 // 3
    s = m // bsz
    hw = 2 * DIFF_DH
    n_heads = d // hw
    tq = min(tq, s)
    assert s % tq == 0
    nq = s // tq
    slopes = jnp.exp2(-8.0 * jnp.arange(1, n_heads + 1, dtype=F32) / n_heads) * LOG2E
    slopes = jnp.broadcast_to(slopes[:, None, None], (n_heads, 1, LANES))
    vec = lambda a: a.reshape(1, -1).astype(F32)
    small = pl.BlockSpec((1, DIFF_DH), lambda b, h, i: (0, 0))
    return pl.pallas_call(
        functools.partial(_diff_attn_kernel, tq=tq, tk=tq, lambda_init=lambda_init),
        grid=(bsz, n_heads, nq),
        in_specs=[pl.BlockSpec((tq, hw), lambda b, h, i: (b * nq + i, h)),
                  pl.BlockSpec((s, hw), lambda b, h, i: (b, n_heads + h)),
                  pl.BlockSpec((s, hw), lambda b, h, i: (b, 2 * n_heads + h)),
                  pl.BlockSpec((1, 1, LANES), lambda b, h, i: (h, 0, 0)),
                  small, small, small, small,
                  pl.BlockSpec((1, hw), lambda b, h, i: (0, 0))],
        out_specs=pl.BlockSpec((tq, hw), lambda b, h, i: (b * nq + i, h)),
        out_shape=jax.ShapeDtypeStruct((m, d), BF16),
        scratch_shapes=[pltpu.VMEM((2, tq, LANES), F32), pltpu.VMEM((2, tq, hw + LANES), F32)],
        compiler_params=_cparams("parallel", "parallel", "arbitrary"),
        name="diff_attn",
    )(proj, proj, proj, slopes, vec(lam_q1), vec(lam_k1), vec(lam_q2), vec(lam_k2), vec(w_subln))


def diff_attn_mixer(hn, bsz, w_in, lam_q1, lam_k1, lam_q2, lam_k2, w_subln, w_out, lambda_init, x_res):
    proj = matmul(hn, w_in.astype(BF16), out_dtype=BF16, tn=1024)
    o = diff_attn_core(proj, bsz, lam_q1, lam_k1, lam_q2, lam_k2, w_subln, lambda_init)
    return matmul(o, w_out.astype(BF16), residual=x_res)


def _mm_conv_silu_kernel(x_ref, w_ref, cw_ref, cb_ref, o_ref, tail_ref, *, width, tiles_per_seq):
    i = pl.program_id(0)
    j = pl.program_id(1)

    @pl.when((i == 0) & (j == 0))
    def _():
        tail_ref[...] = jnp.zeros_like(tail_ref)

    u = jnp.dot(x_ref[...], w_ref[...], preferred_element_type=F32)
    tail = jnp.where(i % tiles_per_seq == 0, 0.0, tail_ref[j])
    y = cw_ref[width - 1:width, :] * u + cb_ref[...]
    for t in range(width - 1):
        y = y + cw_ref[t:t + 1, :] * _shift_rows(u, tail, width - 1 - t)
    tail_ref[j] = u[-SUBLANES:]
    o_ref[...] = (y * jax.nn.sigmoid(y)).astype(o_ref.dtype)


def matmul_conv_silu(x, w, col0, conv_w, conv_b, bsz, tm=1024, tn=512):
    m, k = x.shape
    width, ch = conv_w.shape
    s = m // bsz
    tm = _pick_tile(s, tm, SUBLANES)
    tn = _pick_tile(math.gcd(ch, col0) if col0 else ch, tn, LANES)
    jb0 = col0 // tn
    return pl.pallas_call(
        functools.partial(_mm_conv_silu_kernel, width=width, tiles_per_seq=s // tm),
        grid=(m // tm, ch // tn),
        in_specs=[pl.BlockSpec((tm, k), lambda i, j: (i, 0)),
                  pl.BlockSpec((k, tn), lambda i, j: (0, jb0 + j)),
                  pl.BlockSpec((width, tn), lambda i, j: (0, j)),
                  pl.BlockSpec((1, tn), lambda i, j: (0, j))],
        out_specs=pl.BlockSpec((tm, tn), lambda i, j: (i, j)),
        out_shape=jax.ShapeDtypeStruct((m, ch), BF16),
        scratch_shapes=[pltpu.VMEM((ch // tn, SUBLANES, tn), F32)],
        compiler_params=_cparams("arbitrary", "arbitrary"),
        name="matmul_conv_silu",
    )(x, w, conv_w.astype(F32), conv_b.reshape(1, ch).astype(F32))


def _split3(v):
    hi = v.astype(BF16)
    r = v - hi.astype(F32)
    mid = r.astype(BF16)
    lo = (r - mid.astype(F32)).astype(BF16)
    return hi, mid, lo


def _dot_exact_rhs(lhs_f32, rhs_bf16):
    n = lhs_f32.shape[0]
    out = jnp.dot(jnp.concatenate(_split3(lhs_f32), axis=0), rhs_bf16, preferred_element_type=F32)
    return out[:n] + out[n:2 * n] + out[2 * n:]


def _dot_exact_lhs(lhs_bf16, rhs_f32):
    hi, mid, lo = _split3(rhs_f32)
    return (jnp.dot(lhs_bf16, hi, preferred_element_type=F32) + jnp.dot(lhs_bf16, mid, preferred_element_type=F32)
            + jnp.dot(lhs_bf16, lo, preferred_element_type=F32))


def _softplus(x):
    return jnp.maximum(x, 0.0) + jnp.log1p(jnp.exp(-jnp.abs(x)))


def _ssd_kernel(x_ref, b_ref, c_ref, z_ref, dtc_ref, dtp_ref, dtbc_ref, alc_ref, dtbp_ref, alp_ref, dsk_ref,
                wn_ref, ep_ref, el_ref, o_ref, st_ref, *, L, n_chunks, hpg, p):
    @pl.when(pl.program_id(2) == 0)
    def _():
        st_ref[...] = jnp.zeros_like(st_ref)

    npairs = hpg // 2
    a_c = -jnp.exp(alc_ref[0])
    a_p = -jnp.exp(alp_ref[0])
    tri = (lax.broadcasted_iota(jnp.int32, (L, L), 1) <= lax.broadcasted_iota(jnp.int32, (L, L), 0)).astype(BF16)
    r2 = lax.broadcasted_iota(jnp.int32, (2 * L, 2 * L), 0)
    c2 = lax.broadcasted_iota(jnp.int32, (2 * L, 2 * L), 1)
    bd = ((r2 <= c2) & ((r2 < L) == (c2 < L))).astype(BF16)
    lane2 = lax.broadcasted_iota(jnp.int32, (L, 2 * L), 1)
    row2 = lax.broadcasted_iota(jnp.int32, (L, 2 * L), 0)
    tri2 = jnp.where(lane2 < L, lane2, lane2 - L) <= row2
    lane_p = lax.broadcasted_iota(jnp.int32, (L, 2 * p), 1)

    def chunk(c, carry):
        r0 = pl.multiple_of(c * L, L)
        x = x_ref[pl.ds(r0, L), :].astype(F32)
        bmat = b_ref[pl.ds(r0, L), :]
        cmat = c_ref[pl.ds(r0, L), :]
        dt_c = _softplus(dtc_ref[0, pl.ds(r0, L), :] + dtbc_ref[0])
        cum_c = _dot_exact_lhs(tri, dt_c * a_c)
        p0 = pl.multiple_of(c * npairs, npairs)
        dt_p = _softplus(dtp_ref[0, pl.ds(p0, npairs), :] + dtbp_ref[0])
        cum_p = _dot_exact_rhs(dt_p * a_p, bd)
        both = _dot_exact_rhs(jnp.concatenate([cum_c, dt_c], axis=0), ep_ref[...])
        cum_e, dt_e = both[:L], both[L:]
        cum_l = cum_e if el_ref is None else _dot_exact_rhs(cum_c, el_ref[...])
        cb = lax.dot_general(cmat, bmat, (((1,), (1,)), ((), ())), preferred_element_type=F32)
        cb2 = jnp.concatenate([cb, cb], axis=1)
        x16 = x.astype(BF16)
        ys = []
        for j in range(npairs):
            seg = cum_l[:, j * 2 * L:(j + 1) * 2 * L] - cum_p[j:j + 1, :]
            w = jnp.exp(jnp.where(tri2, seg, NEG_BIG)) * cb2 * dt_p[j:j + 1, :]
            xp = x16[:, j * 2 * p:(j + 1) * 2 * p]
            rhs = jnp.concatenate([jnp.where(lane_p < p, xp, 0), jnp.where(lane_p >= p, xp, 0)], axis=0)
            ys.append(jnp.dot(w.astype(BF16), rhs, preferred_element_type=F32))
        y = jnp.concatenate(ys, axis=1)
        st = st_ref[...]
        y = y + jnp.dot(cmat, st.astype(BF16), preferred_element_type=F32) * jnp.exp(cum_e)
        cum_last = cum_e[L - 1:L, :]
        xw = (x * (jnp.exp(cum_last - cum_e) * dt_e)).astype(BF16)
        st_ref[...] = jnp.exp(cum_last) * st + jnp.dot(bmat.astype(F32).T.astype(BF16), xw,
                                                       preferred_element_type=F32)
        y = y + x * dsk_ref[0]
        zf = z_ref[pl.ds(r0, L), :].astype(F32)
        y = y * (zf * jax.nn.sigmoid(zf))
        y = y * lax.rsqrt(jnp.mean(y * y, axis=1, keepdims=True) + NORM_EPS) * wn_ref[0]
        o_ref[pl.ds(r0, L), :] = y.astype(o_ref.dtype)
        return carry

    lax.fori_loop(0, n_chunks, chunk, 0, unroll=8)


def _kernel_no_el(*refs, **kw):
    *ins, o_ref, st_ref = refs
    _ssd_kernel(*ins, None, o_ref, st_ref, **kw)


def ssd_core(proj, xbc, dt_raw, bsz, dt_bias, a_log, d_skip, w_norm, t_rows=512):
    m = proj.shape[0]
    s = m // bsz
    nh = dt_bias.shape[0]
    p, n, g, L = SSM_HEADDIM, SSM_STATE, SSM_GROUPS, SSM_CHUNK
    di = nh * p
    hpg = nh // g
    gw = hpg * p
    t_rows = min(t_rows, s)
    assert s % t_rows == 0 and t_rows % L == 0 and hpg % 2 == 0 and 2 * L == LANES
    nt = s // t_rows
    n_chunks = t_rows // L
    npairs = hpg // 2
    nc_all = m // L
    dt_c = dt_raw.reshape(m, g, hpg).transpose(1, 0, 2)
    dt_p = (dt_raw.reshape(nc_all, L, g, npairs, 2).transpose(2, 0, 3, 4, 1)
            .reshape(g, nc_all * npairs, 2 * L))
    col = lambda a: a.astype(F32).reshape(g, 1, hpg)
    pair = lambda a: jnp.broadcast_to(a.astype(F32).reshape(g, npairs, 2, 1), (g, npairs, 2, L)).reshape(g, npairs, 2 * L)
    expand = lambda rep: jnp.repeat(jnp.eye(hpg, dtype=BF16), rep, axis=1)
    d_e = jnp.repeat(d_skip.astype(F32), p).reshape(g, 1, gw)
    same = p == L
    row = lambda b, gg, t: b * nt + t
    gspec = lambda shape: pl.BlockSpec((1,) + shape, lambda b, gg, t: (gg, 0, 0))
    in_specs = [pl.BlockSpec((t_rows, gw), lambda b, gg, t: (row(b, gg, t), gg)),
                pl.BlockSpec((t_rows, n), lambda b, gg, t: (row(b, gg, t), di // n + gg)),
                pl.BlockSpec((t_rows, n), lambda b, gg, t: (row(b, gg, t), di // n + g + gg)),
                pl.BlockSpec((t_rows, gw), lambda b, gg, t: (row(b, gg, t), gg)),
                pl.BlockSpec((1, t_rows, hpg), lambda b, gg, t: (gg, row(b, gg, t), 0)),
                pl.BlockSpec((1, n_chunks * npairs, 2 * L), lambda b, gg, t: (gg, row(b, gg, t), 0)),
                gspec((1, hpg)), gspec((1, hpg)), gspec((npairs, 2 * L)), gspec((npairs, 2 * L)),
                gspec((1, gw)), gspec((1, gw)),
                pl.BlockSpec((hpg, gw), lambda b, gg, t: (0, 0))]
    args = [xbc, xbc, xbc, proj, dt_c, dt_p, col(dt_bias), col(a_log), pair(dt_bias), pair(a_log), d_e,
            w_norm.astype(F32).reshape(g, 1, gw), expand(p)]
    if not same:
        in_specs.append(pl.BlockSpec((hpg, hpg * L), lambda b, gg, t: (0, 0)))
        args.append(expand(L))
    kern = functools.partial(_ssd_kernel if not same else _kernel_no_el, L=L, n_chunks=n_chunks, hpg=hpg, p=p)
    return pl.pallas_call(
        kern,
        grid=(bsz, g, nt),
        in_specs=in_specs,
        out_specs=pl.BlockSpec((t_rows, gw), lambda b, gg, t: (row(b, gg, t), gg)),
        out_shape=jax.ShapeDtypeStruct((m, di), BF16),
        scratch_shapes=[pltpu.VMEM((n, gw), F32)],
        compiler_params=_cparams("parallel", "parallel", "arbitrary"),
        name="ssd_core",
    )(*args)


def mamba2_mixer(hn, bsz, w_in, conv_w, conv_b, dt_bias, a_log, d_skip, w_norm, w_out, x_res):
    nh = dt_bias.shape[0]
    di = nh * SSM_HEADDIM
    n_main = w_in.shape[1] - nh
    w_bf = w_in.astype(BF16)
    z = matmul(hn, w_bf, out_dtype=BF16, tn=1024, n_out=di)
    xbc = matmul_conv_silu(hn, w_bf, di, conv_w, conv_b, bsz)
    dt_raw = matmul(hn, w_bf, tn=LANES, col0=n_main, n_out=nh)
    y = ssd_core(z, xbc, dt_raw, bsz, dt_bias, a_log, d_skip, w_norm)
    return matmul(y, w_out.astype(BF16), residual=x_res, tm=512, tk=w_out.shape[0])


def _hgrn2_kernel(q_ref, f_ref, i_ref, g_ref, lb_ref, wn_ref, o_ref, st_ref, *, L, sub, n_chunks, layer):
    @pl.when(pl.program_id(2) == 0)
    def _():
        st_ref[...] = jnp.zeros_like(st_ref)

    dk = q_ref.shape[1]
    lbp = lb_ref[...]
    e = jnp.exp(lbp - jnp.max(lbp, axis=0, keepdims=True))
    sm = e / jnp.sum(e, axis=0, keepdims=True)
    lb = jnp.sum(sm[1:layer + 1], axis=0, keepdims=True) if layer > 0 else jnp.zeros((1, dk), F32)
    log_lb = jnp.log(lb)
    log_1mlb = jnp.log1p(-lb)
    tri = (lax.broadcasted_iota(jnp.int32, (L, L), 1) <= lax.broadcasted_iota(jnp.int32, (L, L), 0)).astype(BF16)
    sub_row = lax.broadcasted_iota(jnp.int32, (sub, dk), 0)
    lane_s = lax.broadcasted_iota(jnp.int32, (sub, sub), 1)
    nsub = L // sub

    def chunk(c, carry):
        r0 = pl.multiple_of(c * L, L)
        qf = q_ref[pl.ds(r0, L), :].astype(F32)
        qa = qf * jax.nn.sigmoid(qf)
        fx = f_ref[pl.ds(r0, L), :].astype(F32)
        key = (1.0 - lb) * jax.nn.sigmoid(-fx)
        log_f = jnp.logaddexp(log_lb, log_1mlb + jax.nn.log_sigmoid(fx))
        b = _dot_exact_lhs(tri, log_f)
        v = i_ref[pl.ds(r0, L), :]
        strips = []
        for blk in range(nsub):
            t0 = blk * sub
            bq = b[t0:t0 + sub]
            qb = qa[t0:t0 + sub]
            cols = []
            if blk > 0:
                ref_row = b[t0 - 1:t0]
                qd = (qb * jnp.exp(bq - ref_row)).astype(BF16)
                kd = (key[:t0] * jnp.exp(ref_row - b[:t0])).astype(BF16)
                cols.append(lax.dot_general(qd, kd, (((1,), (1,)), ((), ())), preferred_element_type=F32))
            diag = jnp.zeros((sub, sub), F32)
            for sl in range(sub):
                srow = t0 + sl
                dec = jnp.exp(jnp.where(sub_row >= sl, bq - b[srow:srow + 1], NEG_BIG))
                col = jnp.sum(qb * dec * key[srow:srow + 1], axis=1, keepdims=True)
                diag = jnp.where(lane_s == sl, col, diag)
            cols.append(diag)
            if t0 + sub < L:
                cols.append(jnp.zeros((sub, L - t0 - sub), F32))
            strips.append(jnp.concatenate(cols, axis=1))
        attn = jnp.concatenate(strips, axis=0)
        st = st_ref[...]
        o = (jnp.dot(attn.astype(BF16), v, preferred_element_type=F32)
             + lax.dot_general((qa * jnp.exp(b)).astype(BF16), st.astype(BF16), (((1,), (1,)), ((), ())),
                               preferred_element_type=F32))
        b_last = b[L - 1:L]
        kd = (key * jnp.exp(b_last - b)).astype(BF16)
        st_ref[...] = jnp.exp(b_last) * st + jnp.dot(v.astype(F32).T.astype(BF16), kd, preferred_element_type=F32)
        o = o * lax.rsqrt(jnp.mean(o * o, axis=1, keepdims=True) + NORM_EPS) * wn_ref[...]
        gf = g_ref[pl.ds(r0, L), :].astype(F32)
        o_ref[pl.ds(r0, L), :] = (o * (gf * jax.nn.sigmoid(gf))).astype(o_ref.dtype)
        return carry

    lax.fori_loop(0, n_chunks, chunk, 0, unroll=16)


def hgrn2_core(proj, bsz, lb_params, layer, w_gnorm, t_rows=1024):
    m, d4 = proj.shape
    d = d4 // 4
    s = m // bsz
    dk, L = HGRN_DK, HGRN_CHUNK
    n_heads = d // dk
    assert d // n_heads == dk, "kernel assumes DV == DK"
    t_rows = min(t_rows, s)
    assert s % t_rows == 0 and t_rows % L == 0
    nt = s // t_rows
    depth = lb_params.shape[0]
    sec = lambda k: pl.BlockSpec((t_rows, dk), lambda b, h, t: (b * nt + t, k * n_heads + h))
    return pl.pallas_call(
        functools.partial(_hgrn2_kernel, L=L, sub=SUBLANES, n_chunks=t_rows // L, layer=layer),
        grid=(bsz, n_heads, nt),
        in_specs=[sec(0), sec(1), sec(2), sec(3),
                  pl.BlockSpec((depth, dk), lambda b, h, t: (0, h)),
                  pl.BlockSpec((1, dk), lambda b, h, t: (0, 0))],
        out_specs=pl.BlockSpec((t_rows, dk), lambda b, h, t: (b * nt + t, h)),
        out_shape=jax.ShapeDtypeStruct((m, d), BF16),
        scratch_shapes=[pltpu.VMEM((dk, dk), F32)],
        compiler_params=_cparams("parallel", "parallel", "arbitrary"),
        name="hgrn2_core",
    )(proj, proj, proj, proj, lb_params.astype(F32), w_gnorm.reshape(1, dk).astype(F32))


def hgrn2_mixer(hn, bsz, w_in, lb_params, layer, w_gnorm, w_out, x_res):
    proj = matmul(hn, w_in.astype(BF16), out_dtype=BF16, tn=1024)
    o = hgrn2_core(proj, bsz, lb_params, layer, w_gnorm)
    return matmul(o, w_out.astype(BF16), residual=x_res)


def conv_ffn(x2d, bsz, g_norm, w_up, conv_w, conv_b, w_down):
    hn = rmsnorm(x2d, g_norm)
    a = ffn_up_conv_act(hn, w_up.astype(BF16), conv_w, conv_b, bsz)
    return matmul(a, w_down.astype(BF16), residual=x2d, tm=512, tk=w_down.shape[0])


def kernel(x, norm_mix, norm_ffn, norm_final, mlstm_w_in, mlstm_b_gates, mlstm_w_norm, mlstm_w_out, diff_w_in, diff_lam_q1, diff_lam_k1, diff_lam_q2, diff_lam_k2, diff_w_subln, diff_w_out, ssm_w_in, ssm_conv_w, ssm_conv_b, ssm_dt_bias, ssm_a_log, ssm_d, ssm_w_norm, ssm_w_out, hgrn_w_in, hgrn_lb, hgrn_w_gnorm, hgrn_w_out, ffn_w_up, ffn_conv_w, ffn_conv_b, ffn_w_down):
    bsz, s, d = x.shape
    depth = norm_mix.shape[0]
    x2d = x.reshape(bsz * s, d)
    for layer in range(depth):
        kind, j = layer % N_MIXERS, layer // N_MIXERS
        hn = rmsnorm(x2d, norm_mix[layer])
        if kind == 0:
            x2d = mlstm_mixer(hn, bsz, mlstm_w_in[j], mlstm_b_gates[j], mlstm_w_norm[j], mlstm_w_out[j], x2d)
        elif kind == 1:
            lambda_init = 0.8 - 0.6 * math.exp(-0.3 * layer)
            x2d = diff_attn_mixer(hn, bsz, diff_w_in[j], diff_lam_q1[j], diff_lam_k1[j], diff_lam_q2[j],
                                  diff_lam_k2[j], diff_w_subln[j], diff_w_out[j], lambda_init, x2d)
        elif kind == 2:
            x2d = mamba2_mixer(hn, bsz, ssm_w_in[j], ssm_conv_w[j], ssm_conv_b[j], ssm_dt_bias[j],
                               ssm_a_log[j], ssm_d[j], ssm_w_norm[j], ssm_w_out[j], x2d)
        else:
            x2d = hgrn2_mixer(hn, bsz, hgrn_w_in[j], hgrn_lb, layer, hgrn_w_gnorm[j], hgrn_w_out[j], x2d)
        x2d = conv_ffn(x2d, bsz, norm_ffn[layer], ffn_w_up[layer], ffn_conv_w[layer], ffn_conv_b[layer],
                       ffn_w_down[layer])
    return rmsnorm(x2d, norm_final, out_dtype=x.dtype).reshape(bsz, s, d)
```

```python
import functools
import math

import jax
import jax.numpy as jnp
from jax import lax
from jax.experimental import pallas as pl
from jax.experimental.pallas import tpu as pltpu

NORM_EPS = 1e-6
N_MIXERS = 4
DIFF_DH = 128
SSM_HEADDIM = 64
SSM_STATE = 128
SSM_GROUPS = 8
SSM_CHUNK = 64
HGRN_DK = 128
HGRN_CHUNK = 64
LOG2E = 1.4426950408889634
NEG_BIG = -1e30

V7X_VMEM_BYTES = 64 * 1024 * 1024
VMEM_LIMIT_BYTES = V7X_VMEM_BYTES - 8 * 1024 * 1024
LANES = 128
SUBLANES = 8

F32 = jnp.float32
BF16 = jnp.bfloat16


def _cparams(*sem):
    return pltpu.CompilerParams(dimension_semantics=sem, vmem_limit_bytes=VMEM_LIMIT_BYTES)


def _lane_tile(x, n):
    return x if n == 1 else jnp.concatenate([x] * n, axis=1)


def _mm_kernel(x_ref, w_ref, *rest, has_res, nk):
    if has_res:
        r_ref, o_ref = rest
    else:
        (o_ref,) = rest
    part = jnp.dot(x_ref[...], w_ref[...], preferred_element_type=F32)
    if nk == 1:
        if has_res:
            part = part + r_ref[...]
        o_ref[...] = part.astype(o_ref.dtype)
    else:
        k = pl.program_id(2)

        @pl.when(k == 0)
        def _():
            if has_res:
                o_ref[...] = r_ref[...] + part
            else:
                o_ref[...] = part

        @pl.when(k > 0)
        def _():
            o_ref[...] += part


def _pick_tile(n, target, quantum):
    best = None
    t = quantum
    while t <= min(n, target):
        if n % t == 0:
            best = t
        t += quantum
    assert best is not None, (n, target, quantum)
    return best


def matmul(x, w, residual=None, out_dtype=F32, tm=1024, tn=512, tk=4096, col0=0, n_out=None):
    m, k = x.shape
    k2 = w.shape[0]
    n = w.shape[1] - col0 if n_out is None else n_out
    assert k == k2
    tm = _pick_tile(m, tm, SUBLANES)
    tn = _pick_tile(math.gcd(n, col0) if col0 else n, tn, LANES)
    tk = k if k <= tk else _pick_tile(k, max(tk, 5504), LANES)
    nk = k // tk
    if nk > 1:
        assert out_dtype == F32
    has_res = residual is not None
    jb0 = col0 // tn
    in_specs = [pl.BlockSpec((tm, tk), lambda i, j, kk: (i, kk)),
                pl.BlockSpec((tk, tn), lambda i, j, kk: (kk, jb0 + j))]
    args = [x, w]
    if has_res:
        in_specs.append(pl.BlockSpec((tm, tn), lambda i, j, kk: (i, j)))
        args.append(residual)
    return pl.pallas_call(
        functools.partial(_mm_kernel, has_res=has_res, nk=nk),
        grid=(m // tm, n // tn, nk),
        in_specs=in_specs,
        out_specs=pl.BlockSpec((tm, tn), lambda i, j, kk: (i, j)),
        out_shape=jax.ShapeDtypeStruct((m, n), out_dtype),
        compiler_params=_cparams("parallel", "parallel", "arbitrary"),
        name="matmul",
    )(*args)


def _rmsnorm_kernel(x_ref, g_ref, o_ref):
    x = x_ref[...].astype(F32)
    ms = jnp.mean(x * x, axis=-1, keepdims=True)
    o_ref[...] = (x * lax.rsqrt(ms + NORM_EPS) * g_ref[...]).astype(o_ref.dtype)


def rmsnorm(x, g, out_dtype=BF16, tm=512):
    m, d = x.shape
    tm = _pick_tile(m, tm, SUBLANES)
    return pl.pallas_call(
        _rmsnorm_kernel,
        grid=(m // tm,),
        in_specs=[pl.BlockSpec((tm, d), lambda i: (i, 0)),
                  pl.BlockSpec((1, d), lambda i: (0, 0))],
        out_specs=pl.BlockSpec((tm, d), lambda i: (i, 0)),
        out_shape=jax.ShapeDtypeStruct((m, d), out_dtype),
        compiler_params=_cparams("parallel"),
        name="rmsnorm",
    )(x, g.reshape(1, d).astype(F32))


def _shift_rows(cur, prev_tail, shift):
    rolled = pltpu.roll(cur, shift, axis=0)
    tail = pltpu.roll(prev_tail, shift, axis=0)
    row = lax.broadcasted_iota(jnp.int32, (SUBLANES, cur.shape[1]), 0)
    head = jnp.where(row < shift, tail, rolled[:SUBLANES])
    return jnp.concatenate([head, rolled[SUBLANES:]], axis=0)


def _ffn_up_kernel(x_ref, wg_ref, wv_ref, cg_ref, cv_ref, bg_ref, bv_ref, o_ref,
                   ug_ref, uv_ref, tg_ref, tv_ref, *, tiles_per_seq, ncb):
    t = pl.program_id(0)

    @pl.when(t == 0)
    def _():
        for r in (ug_ref, uv_ref, tg_ref, tv_ref):
            r[...] = jnp.zeros_like(r)

    x = x_ref[...]
    ug_new = jnp.dot(x, wg_ref[...], preferred_element_type=F32)
    uv_new = jnp.dot(x, wv_ref[...], preferred_element_type=F32)
    prev = jnp.maximum(t - 1, 0)
    pj = prev % ncb
    seq_start = (prev // ncb) % tiles_per_seq == 0

    def conv(u, t_ref, w_ref, b_ref):
        old = t_ref[pj]
        tail = jnp.where(seq_start, 0.0, old)
        y = (w_ref[2:3, :] * u + w_ref[1:2, :] * _shift_rows(u, tail, 1)
             + w_ref[0:1, :] * _shift_rows(u, tail, 2) + b_ref[...])
        t_ref[pj] = jnp.where(t > 0, u[-SUBLANES:], old)
        return y

    g = conv(ug_ref[...], tg_ref, cg_ref, bg_ref)
    v = conv(uv_ref[...], tv_ref, cv_ref, bv_ref)
    o_ref[...] = (g * jax.nn.sigmoid(g) * v).astype(o_ref.dtype)
    ug_ref[...] = ug_new
    uv_ref[...] = uv_new


def ffn_up_conv_act(hn, w_up, conv_w, conv_b, bsz, tm=1024, tn=256):
    m, k = hn.shape
    f = w_up.shape[1] // 2
    s = m // bsz
    tm = _pick_tile(s, tm, SUBLANES)
    tn = _pick_tile(f, tn, LANES)
    ncb = f // tn
    n_tiles = (m // tm) * ncb
    width = conv_w.shape[0]
    assert width == 3
    cw = conv_w.astype(F32)
    cb = conv_b.reshape(1, 2 * f).astype(F32)
    cur = lambda t: jnp.minimum(t, n_tiles - 1)
    prev = lambda t: jnp.maximum(t - 1, 0)
    return pl.pallas_call(
        functools.partial(_ffn_up_kernel, tiles_per_seq=s // tm, ncb=ncb),
        grid=(n_tiles + 1,),
        in_specs=[pl.BlockSpec((tm, k), lambda t: (cur(t) // ncb, 0)),
                  pl.BlockSpec((k, tn), lambda t: (0, cur(t) % ncb)),
                  pl.BlockSpec((k, tn), lambda t: (0, cur(t) % ncb + ncb)),
                  pl.BlockSpec((width, tn), lambda t: (0, prev(t) % ncb)),
                  pl.BlockSpec((width, tn), lambda t: (0, prev(t) % ncb + ncb)),
                  pl.BlockSpec((1, tn), lambda t: (0, prev(t) % ncb)),
                  pl.BlockSpec((1, tn), lambda t: (0, prev(t) % ncb + ncb))],
        out_specs=pl.BlockSpec((tm, tn), lambda t: (prev(t) // ncb, prev(t) % ncb)),
        out_shape=jax.ShapeDtypeStruct((m, f), BF16),
        scratch_shapes=[pltpu.VMEM((tm, tn), F32), pltpu.VMEM((tm, tn), F32),
                        pltpu.VMEM((ncb, SUBLANES, tn), F32), pltpu.VMEM((ncb, SUBLANES, tn), F32)],
        compiler_params=_cparams("arbitrary"),
        name="ffn_up_conv_act",
    )(hn, w_up, w_up, cw, cw, cb, cb)


def _lane_cumsum(x):
    lane = lax.broadcasted_iota(jnp.int32, x.shape, 1)
    shift = 1
    while shift < x.shape[1]:
        x = x + jnp.where(lane >= shift, pltpu.roll(x, shift, axis=1), 0.0)
        shift *= 2
    return x


def _mlstm_kernel(q_ref, k_ref, v_ref, og_ref, gi_ref, gf_ref, bi_ref, bf_ref, wn_ref, out_ref,
                  c_ref, n_ref, m_ref, *, n_chunks):
    L = LANES
    dk = q_ref.shape[1]
    dv = v_ref.shape[1]

    @pl.when(pl.program_id(2) == 0)
    def _():
        c_ref[...] = jnp.zeros_like(c_ref)
        n_ref[...] = jnp.zeros_like(n_ref)
        m_ref[...] = jnp.zeros_like(m_ref)

    row_i = lax.broadcasted_iota(jnp.int32, (L, L), 0)
    col_i = lax.broadcasted_iota(jnp.int32, (L, L), 1)
    causal = col_i <= row_i
    bias_i = bi_ref[0]
    bias_f = bf_ref[0]

    for c in range(n_chunks):
        rows = slice(c * L, (c + 1) * L)
        q = (q_ref[rows, :].astype(F32) * (dk ** -0.5)).astype(BF16)
        k = k_ref[rows, :]
        v = v_ref[rows, :]
        i_row = gi_ref[0, :, rows] + bias_i
        f_raw = gf_ref[0, :, rows] + bias_f
        lf_row = jnp.minimum(f_raw, 0.0) - jnp.log1p(jnp.exp(-jnp.abs(f_raw)))
        b_row = _lane_cumsum(lf_row)
        b_last = jnp.min(b_row, axis=1, keepdims=True)
        bm = jnp.broadcast_to(b_row, (L, L))
        im = jnp.broadcast_to(i_row, (L, L))
        bt = bm.T
        it = im.T
        m_prev = m_ref[0:1, :]
        d = jnp.where(causal, bt - bm + im, NEG_BIG)
        a = bt + m_prev
        m_t = jnp.maximum(a, jnp.max(d, axis=1, keepdims=True))
        w_intra = jnp.exp(d - m_t)
        w_inter = jnp.exp(a - m_t)
        sc = lax.dot_general(q, k, (((1,), (1,)), ((), ())), preferred_element_type=F32) * w_intra
        num = (jnp.dot(sc.astype(BF16), v, preferred_element_type=F32)
               + _lane_tile(w_inter, dv // LANES)
               * jnp.dot(q, c_ref[...].astype(BF16), preferred_element_type=F32))
        qn = jnp.sum(q.astype(F32) * n_ref[0:1, :], axis=1, keepdims=True)
        den = jnp.sum(sc, axis=1, keepdims=True) + w_inter * qn
        den = jnp.maximum(jnp.abs(den), jnp.exp(-m_t))
        hc = num / _lane_tile(den, dv // LANES)
        hc = hc * lax.rsqrt(jnp.mean(hc * hc, axis=1, keepdims=True) + NORM_EPS) * wn_ref[...]
        out_ref[rows, :] = (jax.nn.sigmoid(og_ref[rows, :].astype(F32)) * hc).astype(out_ref.dtype)

        g = b_last - bt + it
        m_new = jnp.maximum(b_last + m_prev, jnp.max(g, axis=0, keepdims=True))
        ws = jnp.exp(g - m_new)
        decay = jnp.exp(b_last + m_prev - m_new)
        kw = k.astype(F32) * _lane_tile(ws, dk // LANES)
        c_ref[...] = (_lane_tile(decay, dv // LANES) * c_ref[...]
                      + jnp.dot(kw.T.astype(BF16), v, preferred_element_type=F32))
        n_ref[...] = (_lane_tile(decay, dk // LANES) * n_ref[...]
                      + jnp.sum(kw, axis=0, keepdims=True))
        m_ref[...] = jnp.broadcast_to(m_new, m_ref.shape)


def mlstm_core(proj, gates_t, bsz, b_gates, w_norm, t_rows=1024):
    m = proj.shape[0]
    s = m // bsz
    n_heads = b_gates.shape[0] // 2
    dv = w_norm.shape[0] // n_heads
    dk = dv // 2
    t_rows = min(t_rows, s)
    assert s % t_rows == 0 and t_rows % LANES == 0 and dk % LANES == 0
    nt = s // t_rows
    bg = jnp.broadcast_to(b_gates.astype(F32)[:, None, None], (2 * n_heads, 1, LANES))
    row = lambda b, h, t: b * nt + t
    return pl.pallas_call(
        functools.partial(_mlstm_kernel, n_chunks=t_rows // LANES),
        grid=(bsz, n_heads, nt),
        in_specs=[pl.BlockSpec((t_rows, dk), lambda b, h, t: (row(b, h, t), h)),
                  pl.BlockSpec((t_rows, dk), lambda b, h, t: (row(b, h, t), n_heads + h)),
                  pl.BlockSpec((t_rows, dv), lambda b, h, t: (row(b, h, t), n_heads + h)),
                  pl.BlockSpec((t_rows, dv), lambda b, h, t: (row(b, h, t), 2 * n_heads + h)),
                  pl.BlockSpec((1, 1, t_rows), lambda b, h, t: (h, 0, row(b, h, t))),
                  pl.BlockSpec((1, 1, t_rows), lambda b, h, t: (n_heads + h, 0, row(b, h, t))),
                  pl.BlockSpec((1, 1, LANES), lambda b, h, t: (h, 0, 0)),
                  pl.BlockSpec((1, 1, LANES), lambda b, h, t: (n_heads + h, 0, 0)),
                  pl.BlockSpec((1, dv), lambda b, h, t: (0, h))],
        out_specs=pl.BlockSpec((t_rows, dv), lambda b, h, t: (row(b, h, t), h)),
        out_shape=jax.ShapeDtypeStruct((m, n_heads * dv), BF16),
        scratch_shapes=[pltpu.VMEM((dk, dv), F32), pltpu.VMEM((SUBLANES, dk), F32),
                        pltpu.VMEM((SUBLANES, LANES), F32)],
        compiler_params=_cparams("parallel", "parallel", "arbitrary"),
        name="mlstm_core",
    )(proj, proj, proj, proj, gates_t, gates_t, bg, bg, w_norm.reshape(1, n_heads * dv).astype(F32))


def mlstm_mixer(hn, bsz, w_in, b_gates, w_norm, w_out, x_res):
    n_heads = b_gates.shape[0] // 2
    n_main = w_in.shape[1] - 2 * n_heads
    w_g = jnp.pad(w_in[:, n_main:], ((0, 0), (0, LANES - 2 * n_heads))).astype(BF16)
    proj = matmul(hn, w_in.astype(BF16), out_dtype=BF16, tn=1024, n_out=n_main)
    gates_t = matmul(hn, w_g, tn=LANES)[:, :2 * n_heads].T[:, None, :]
    hh = mlstm_core(proj, gates_t, bsz, b_gates, w_norm)
    return matmul(hh, w_out.astype(BF16), residual=x_res)


def _diff_attn_kernel(q_ref, k_ref, v_ref, sl_ref, lq1_ref, lk1_ref, lq2_ref, lk2_ref, g_ref, o_ref,
                      m_ref, acc_ref, *, tq, tk, lambda_init):
    qi = pl.program_id(2)
    dh = DIFF_DH
    q = (q_ref[...].astype(F32) * (dh ** -0.5 * LOG2E)).astype(BF16)
    slope_row = _lane_tile(sl_ref[0], tk // LANES)
    col = lax.broadcasted_iota(jnp.int32, (1, tk), 1).astype(F32)
    m_ref[...] = jnp.full(m_ref.shape, NEG_BIG, F32)
    acc_ref[...] = jnp.zeros(acc_ref.shape, F32)

    def step(kk, masked):
        start = pl.multiple_of(kk * tk, tk)
        cb = (col + (kk * tk - qi * tq).astype(F32)) * slope_row
        kblk = k_ref[pl.ds(start, tk), :]
        vblk = jnp.concatenate([v_ref[pl.ds(start, tk), :], jnp.ones((tk, LANES), BF16)], axis=1)
        for c in range(2):
            s = lax.dot_general(q[:, c * dh:(c + 1) * dh], kblk[:, c * dh:(c + 1) * dh],
                                (((1,), (1,)), ((), ())), preferred_element_type=F32)
            s = s + cb
            if masked:
                row_i = lax.broadcasted_iota(jnp.int32, (tq, tk), 0)
                col_i = lax.broadcasted_iota(jnp.int32, (tq, tk), 1)
                s = jnp.where(col_i <= row_i, s, NEG_BIG)
            m_prev = m_ref[c]
            m_new = jnp.maximum(m_prev, jnp.max(s, axis=1, keepdims=True))
            alpha = jnp.exp2(m_prev - m_new)
            p = jnp.exp2(s - _lane_tile(m_new, tk // LANES))
            m_ref[c] = m_new
            acc_ref[c] = (acc_ref[c] * _lane_tile(alpha, 2 * dh // LANES + 1)
                          + jnp.dot(p.astype(BF16), vblk, preferred_element_type=F32))

    def body(kk, carry):
        step(kk, False)
        return carry

    lax.fori_loop(0, qi, body, 0)
    step(qi, True)

    lam = (jnp.exp(jnp.sum(lq1_ref[...] * lk1_ref[...])) - jnp.exp(jnp.sum(lq2_ref[...] * lk2_ref[...]))
           + lambda_init)
    rep = 2 * dh // LANES
    o = (acc_ref[0, :, :2 * dh] / _lane_tile(acc_ref[0, :, 2 * dh:], rep)
         - lam * (acc_ref[1, :, :2 * dh] / _lane_tile(acc_ref[1, :, 2 * dh:], rep)))
    o = o * lax.rsqrt(jnp.mean(o * o, axis=-1, keepdims=True) + NORM_EPS) * g_ref[...]
    o_ref[...] = (o * (1.0 - lambda_init)).astype(o_ref.dtype)


def diff_attn_core(proj, bsz, lam_q1, lam_k1, lam_q2, lam_k2, w_subln, lambda_init, tq=1024):
    m, d3 = proj.shape
    d = d3 // 3
    s = m // bsz
    hw = 2 * DIFF_DH
    n_heads = d // hw
    tq = min(tq, s)
    assert s % tq == 0
    nq = s // tq
    slopes = jnp.exp2(-8.0 * jnp.arange(1, n_heads + 1, dtype=F32) / n_heads) * LOG2E
    slopes = jnp.broadcast_to(slopes[:, None, None], (n_heads, 1, LANES))
    vec = lambda a: a.reshape(1, -1).astype(F32)
    small = pl.BlockSpec((1, DIFF_DH), lambda b, h, i: (0, 0))
    return pl.pallas_call(
        functools.partial(_diff_attn_kernel, tq=tq, tk=tq, lambda_init=lambda_init),
        grid=(bsz, n_heads, nq),
        in_specs=[pl.BlockSpec((tq, hw), lambda b, h, i: (b * nq + i, h)),
                  pl.BlockSpec((s, hw), lambda b, h, i: (b, n_heads + h)),
                  pl.BlockSpec((s, hw), lambda b, h, i: (b, 2 * n_heads + h)),
                  pl.BlockSpec((1, 1, LANES), lambda b, h, i: (h, 0, 0)),
                  small, small, small, small,
                  pl.BlockSpec((1, hw), lambda b, h, i: (0, 0))],
        out_specs=pl.BlockSpec((tq, hw), lambda b, h, i: (b * nq + i, h)),
        out_shape=jax.ShapeDtypeStruct((m, d), BF16),
        scratch_shapes=[pltpu.VMEM((2, tq, LANES), F32), pltpu.VMEM((2, tq, hw + LANES), F32)],
        compiler_params=_cparams("parallel", "parallel", "arbitrary"),
        name="diff_attn",
    )(proj, proj, proj, slopes, vec(lam_q1), vec(lam_k1), vec(lam_q2), vec(lam_k2), vec(w_subln))


def diff_attn_mixer(hn, bsz, w_in, lam_q1, lam_k1, lam_q2, lam_k2, w_subln, w_out, lambda_init, x_res):
    proj = matmul(hn, w_in.astype(BF16), out_dtype=BF16, tn=1024)
    o = diff_attn_core(proj, bsz, lam_q1, lam_k1, lam_q2, lam_k2, w_subln, lambda_init)
    return matmul(o, w_out.astype(BF16), residual=x_res)


def _mm_conv_silu_kernel(x_ref, w_ref, cw_ref, cb_ref, o_ref, tail_ref, *, width, tiles_per_seq):
    i = pl.program_id(0)
    j = pl.program_id(1)

    @pl.when((i == 0) & (j == 0))
    def _():
        tail_ref[...] = jnp.zeros_like(tail_ref)

    u = jnp.dot(x_ref[...], w_ref[...], preferred_element_type=F32)
    tail = jnp.where(i % tiles_per_seq == 0, 0.0, tail_ref[j])
    y = cw_ref[width - 1:width, :] * u + cb_ref[...]
    for t in range(width - 1):
        y = y + cw_ref[t:t + 1, :] * _shift_rows(u, tail, width - 1 - t)
    tail_ref[j] = u[-SUBLANES:]
    o_ref[...] = (y * jax.nn.sigmoid(y)).astype(o_ref.dtype)


def matmul_conv_silu(x, w, col0, conv_w, conv_b, bsz, tm=1024, tn=1024):
    m, k = x.shape
    width, ch = conv_w.shape
    s = m // bsz
    tm = _pick_tile(s, tm, SUBLANES)
    tn = _pick_tile(math.gcd(ch, col0) if col0 else ch, tn, LANES)
    jb0 = col0 // tn
    return pl.pallas_call(
        functools.partial(_mm_conv_silu_kernel, width=width, tiles_per_seq=s // tm),
        grid=(m // tm, ch // tn),
        in_specs=[pl.BlockSpec((tm, k), lambda i, j: (i, 0)),
                  pl.BlockSpec((k, tn), lambda i, j: (0, jb0 + j)),
                  pl.BlockSpec((width, tn), lambda i, j: (0, j)),
                  pl.BlockSpec((1, tn), lambda i, j: (0, j))],
        out_specs=pl.BlockSpec((tm, tn), lambda i, j: (i, j)),
        out_shape=jax.ShapeDtypeStruct((m, ch), BF16),
        scratch_shapes=[pltpu.VMEM((ch // tn, SUBLANES, tn), F32)],
        compiler_params=_cparams("arbitrary", "arbitrary"),
        name="matmul_conv_silu",
    )(x, w, conv_w.astype(F32), conv_b.reshape(1, ch).astype(F32))


def _split3(v):
    hi = v.astype(BF16)
    r = v - hi.astype(F32)
    mid = r.astype(BF16)
    lo = (r - mid.astype(F32)).astype(BF16)
    return hi, mid, lo


def _dot_exact_rhs(lhs_f32, rhs_bf16):
    n = lhs_f32.shape[0]
    out = jnp.dot(jnp.concatenate(_split3(lhs_f32), axis=0), rhs_bf16, preferred_element_type=F32)
    return out[:n] + out[n:2 * n] + out[2 * n:]


def _dot_exact_lhs(lhs_bf16, rhs_f32):
    hi, mid, lo = _split3(rhs_f32)
    return (jnp.dot(lhs_bf16, hi, preferred_element_type=F32) + jnp.dot(lhs_bf16, mid, preferred_element_type=F32)
            + jnp.dot(lhs_bf16, lo, preferred_element_type=F32))


def _softplus(x):
    return jnp.maximum(x, 0.0) + jnp.log1p(jnp.exp(-jnp.abs(x)))


def _ssd_kernel(x_ref, b_ref, c_ref, z_ref, dtc_ref, dtp_ref, dtbc_ref, alc_ref, dtbp_ref, alp_ref, dsk_ref,
                wn_ref, ep_ref, el_ref, o_ref, st_ref, *, L, n_chunks, hpg, p):
    @pl.when(pl.program_id(2) == 0)
    def _():
        st_ref[...] = jnp.zeros_like(st_ref)

    npairs = hpg // 2
    a_c = -jnp.exp(alc_ref[0])
    a_p = -jnp.exp(alp_ref[0])
    tri = (lax.broadcasted_iota(jnp.int32, (L, L), 1) <= lax.broadcasted_iota(jnp.int32, (L, L), 0)).astype(BF16)
    r2 = lax.broadcasted_iota(jnp.int32, (2 * L, 2 * L), 0)
    c2 = lax.broadcasted_iota(jnp.int32, (2 * L, 2 * L), 1)
    bd = ((r2 <= c2) & ((r2 < L) == (c2 < L))).astype(BF16)
    lane2 = lax.broadcasted_iota(jnp.int32, (L, 2 * L), 1)
    row2 = lax.broadcasted_iota(jnp.int32, (L, 2 * L), 0)
    tri2 = jnp.where(lane2 < L, lane2, lane2 - L) <= row2
    lane_p = lax.broadcasted_iota(jnp.int32, (L, 2 * p), 1)

    def chunk(c, carry):
        r0 = pl.multiple_of(c * L, L)
        x = x_ref[pl.ds(r0, L), :].astype(F32)
        bmat = b_ref[pl.ds(r0, L), :]
        cmat = c_ref[pl.ds(r0, L), :]
        dt_c = _softplus(dtc_ref[0, pl.ds(r0, L), :] + dtbc_ref[0])
        cum_c = _dot_exact_lhs(tri, dt_c * a_c)
        p0 = pl.multiple_of(c * npairs, npairs)
        dt_p = _softplus(dtp_ref[0, pl.ds(p0, npairs), :] + dtbp_ref[0])
        cum_p = _dot_exact_rhs(dt_p * a_p, bd)
        both = _dot_exact_rhs(jnp.concatenate([cum_c, dt_c], axis=0), ep_ref[...])
        cum_e, dt_e = both[:L], both[L:]
        cum_l = cum_e if el_ref is None else _dot_exact_rhs(cum_c, el_ref[...])
        cb = lax.dot_general(cmat, bmat, (((1,), (1,)), ((), ())), preferred_element_type=F32)
        cb2 = jnp.concatenate([cb, cb], axis=1)
        x16 = x.astype(BF16)
        ys = []
        for j in range(npairs):
            seg = cum_l[:, j * 2 * L:(j + 1) * 2 * L] - cum_p[j:j + 1, :]
            w = jnp.exp(jnp.where(tri2, seg, NEG_BIG)) * cb2 * dt_p[j:j + 1, :]
            xp = x16[:, j * 2 * p:(j + 1) * 2 * p]
            rhs = jnp.concatenate([jnp.where(lane_p < p, xp, 0), jnp.where(lane_p >= p, xp, 0)], axis=0)
            ys.append(jnp.dot(w.astype(BF16), rhs, preferred_element_type=F32))
        y = jnp.concatenate(ys, axis=1)
        st = st_ref[...]
        y = y + jnp.dot(cmat, st.astype(BF16), preferred_element_type=F32) * jnp.exp(cum_e)
        cum_last = cum_e[L - 1:L, :]
        xw = (x * (jnp.exp(cum_last - cum_e) * dt_e)).astype(BF16)
        st_ref[...] = jnp.exp(cum_last) * st + jnp.dot(bmat.astype(F32).T.astype(BF16), xw,
                                                       preferred_element_type=F32)
        y = y + x * dsk_ref[0]
        zf = z_ref[pl.ds(r0, L), :].astype(F32)
        y = y * (zf * jax.nn.sigmoid(zf))
        y = y * lax.rsqrt(jnp.mean(y * y, axis=1, keepdims=True) + NORM_EPS) * wn_ref[0]
        o_ref[pl.ds(r0, L), :] = y.astype(o_ref.dtype)
        return carry

    lax.fori_loop(0, n_chunks, chunk, 0, unroll=8)


def _kernel_no_el(*refs, **kw):
    *ins, o_ref, st_ref = refs
    _ssd_kernel(*ins, None, o_ref, st_ref, **kw)


def ssd_core(proj, xbc, dt_raw, bsz, dt_bias, a_log, d_skip, w_norm, t_rows=512):
    m = proj.shape[0]
    s = m // bsz
    nh = dt_bias.shape[0]
    p, n, g, L = SSM_HEADDIM, SSM_STATE, SSM_GROUPS, SSM_CHUNK
    di = nh * p
    hpg = nh // g
    gw = hpg * p
    t_rows = min(t_rows, s)
    assert s % t_rows == 0 and t_rows % L == 0 and hpg % 2 == 0 and 2 * L == LANES
    nt = s // t_rows
    n_chunks = t_rows // L
    npairs = hpg // 2
    nc_all = m // L
    dt_c = dt_raw.reshape(m, g, hpg).transpose(1, 0, 2)
    dt_p = (dt_raw.reshape(nc_all, L, g, npairs, 2).transpose(2, 0, 3, 4, 1)
            .reshape(g, nc_all * npairs, 2 * L))
    col = lambda a: a.astype(F32).reshape(g, 1, hpg)
    pair = lambda a: jnp.broadcast_to(a.astype(F32).reshape(g, npairs, 2, 1), (g, npairs, 2, L)).reshape(g, npairs, 2 * L)
    expand = lambda rep: jnp.repeat(jnp.eye(hpg, dtype=BF16), rep, axis=1)
    d_e = jnp.repeat(d_skip.astype(F32), p).reshape(g, 1, gw)
    same = p == L
    row = lambda b, gg, t: b * nt + t
    gspec = lambda shape: pl.BlockSpec((1,) + shape, lambda b, gg, t: (gg, 0, 0))
    in_specs = [pl.BlockSpec((t_rows, gw), lambda b, gg, t: (row(b, gg, t), gg)),
                pl.BlockSpec((t_rows, n), lambda b, gg, t: (row(b, gg, t), di // n + gg)),
                pl.BlockSpec((t_rows, n), lambda b, gg, t: (row(b, gg, t), di // n + g + gg)),
                pl.BlockSpec((t_rows, gw), lambda b, gg, t: (row(b, gg, t), gg)),
                pl.BlockSpec((1, t_rows, hpg), lambda b, gg, t: (gg, row(b, gg, t), 0)),
                pl.BlockSpec((1, n_chunks * npairs, 2 * L), lambda b, gg, t: (gg, row(b, gg, t), 0)),
                gspec((1, hpg)), gspec((1, hpg)), gspec((npairs, 2 * L)), gspec((npairs, 2 * L)),
                gspec((1, gw)), gspec((1, gw)),
                pl.BlockSpec((hpg, gw), lambda b, gg, t: (0, 0))]
    args = [xbc, xbc, xbc, proj, dt_c, dt_p, col(dt_bias), col(a_log), pair(dt_bias), pair(a_log), d_e,
            w_norm.astype(F32).reshape(g, 1, gw), expand(p)]
    if not same:
        in_specs.append(pl.BlockSpec((hpg, hpg * L), lambda b, gg, t: (0, 0)))
        args.append(expand(L))
    kern = functools.partial(_ssd_kernel if not same else _kernel_no_el, L=L, n_chunks=n_chunks, hpg=hpg, p=p)
    return pl.pallas_call(
        kern,
        grid=(bsz, g, nt),
        in_specs=in_specs,
        out_specs=pl.BlockSpec((t_rows, gw), lambda b, gg, t: (row(b, gg, t), gg)),
        out_shape=jax.ShapeDtypeStruct((m, di), BF16),
        scratch_shapes=[pltpu.VMEM((n, gw), F32)],
        compiler_params=_cparams("parallel", "parallel", "arbitrary"),
        name="ssd_core",
    )(*args)


def mamba2_mixer(hn, bsz, w_in, conv_w, conv_b, dt_bias, a_log, d_skip, w_norm, w_out, x_res):
    nh = dt_bias.shape[0]
    di = nh * SSM_HEADDIM
    n_main = w_in.shape[1] - nh
    w_bf = w_in.astype(BF16)
    z = matmul(hn, w_bf, out_dtype=BF16, tn=1024, n_out=di)
    xbc = matmul_conv_silu(hn, w_bf, di, conv_w, conv_b, bsz)
    dt_raw = matmul(hn, w_bf, tn=LANES, col0=n_main, n_out=nh)
    y = ssd_core(z, xbc, dt_raw, bsz, dt_bias, a_log, d_skip, w_norm)
    return matmul(y, w_out.astype(BF16), residual=x_res, tm=512, tk=w_out.shape[0])


def _hgrn2_kernel(q_ref, f_ref, i_ref, g_ref, lb_ref, wn_ref, o_ref, st_ref, *, L, sub, n_chunks, layer):
    @pl.when(pl.program_id(2) == 0)
    def _():
        st_ref[...] = jnp.zeros_like(st_ref)

    dk = q_ref.shape[1]
    lbp = lb_ref[...]
    e = jnp.exp(lbp - jnp.max(lbp, axis=0, keepdims=True))
    sm = e / jnp.sum(e, axis=0, keepdims=True)
    lb = jnp.sum(sm[1:layer + 1], axis=0, keepdims=True) if layer > 0 else jnp.zeros((1, dk), F32)
    log_lb = jnp.log(lb)
    log_1mlb = jnp.log1p(-lb)
    tri = (lax.broadcasted_iota(jnp.int32, (L, L), 1) <= lax.broadcasted_iota(jnp.int32, (L, L), 0)).astype(BF16)
    sub_row = lax.broadcasted_iota(jnp.int32, (sub, dk), 0)
    lane_s = lax.broadcasted_iota(jnp.int32, (sub, sub), 1)
    nsub = L // sub

    def chunk(c, carry):
        r0 = pl.multiple_of(c * L, L)
        qf = q_ref[pl.ds(r0, L), :].astype(F32)
        qa = qf * jax.nn.sigmoid(qf)
        fx = f_ref[pl.ds(r0, L), :].astype(F32)
        key = (1.0 - lb) * jax.nn.sigmoid(-fx)
        log_f = jnp.logaddexp(log_lb, log_1mlb + jax.nn.log_sigmoid(fx))
        b = _dot_exact_lhs(tri, log_f)
        v = i_ref[pl.ds(r0, L), :]
        strips = []
        for blk in range(nsub):
            t0 = blk * sub
            bq = b[t0:t0 + sub]
            qb = qa[t0:t0 + sub]
            cols = []
            if blk > 0:
                ref_row = b[t0 - 1:t0]
                qd = (qb * jnp.exp(bq - ref_row)).astype(BF16)
                kd = (key[:t0] * jnp.exp(ref_row - b[:t0])).astype(BF16)
                cols.append(lax.dot_general(qd, kd, (((1,), (1,)), ((), ())), preferred_element_type=F32))
            diag = jnp.zeros((sub, sub), F32)
            for sl in range(sub):
                srow = t0 + sl
                dec = jnp.exp(jnp.where(sub_row >= sl, bq - b[srow:srow + 1], NEG_BIG))
                col = jnp.sum(qb * dec * key[srow:srow + 1], axis=1, keepdims=True)
                diag = jnp.where(lane_s == sl, col, diag)
            cols.append(diag)
            if t0 + sub < L:
                cols.append(jnp.zeros((sub, L - t0 - sub), F32))
            strips.append(jnp.concatenate(cols, axis=1))
        attn = jnp.concatenate(strips, axis=0)
        st = st_ref[...]
        o = (jnp.dot(attn.astype(BF16), v, preferred_element_type=F32)
             + lax.dot_general((qa * jnp.exp(b)).astype(BF16), st.astype(BF16), (((1,), (1,)), ((), ())),
                               preferred_element_type=F32))
        b_last = b[L - 1:L]
        kd = (key * jnp.exp(b_last - b)).astype(BF16)
        st_ref[...] = jnp.exp(b_last) * st + jnp.dot(v.astype(F32).T.astype(BF16), kd, preferred_element_type=F32)
        o = o * lax.rsqrt(jnp.mean(o * o, axis=1, keepdims=True) + NORM_EPS) * wn_ref[...]
        gf = g_ref[pl.ds(r0, L), :].astype(F32)
        o_ref[pl.ds(r0, L), :] = (o * (gf * jax.nn.sigmoid(gf))).astype(o_ref.dtype)
        return carry

    lax.fori_loop(0, n_chunks, chunk, 0, unroll=16)


def hgrn2_core(proj, bsz, lb_params, layer, w_gnorm, t_rows=1024):
    m, d4 = proj.shape
    d = d4 // 4
    s = m // bsz
    dk, L = HGRN_DK, HGRN_CHUNK
    n_heads = d // dk
    assert d // n_heads == dk, "kernel assumes DV == DK"
    t_rows = min(t_rows, s)
    assert s % t_rows == 0 and t_rows % L == 0
    nt = s // t_rows
    depth = lb_params.shape[0]
    sec = lambda k: pl.BlockSpec((t_rows, dk), lambda b, h, t: (b * nt + t, k * n_heads + h))
    return pl.pallas_call(
        functools.partial(_hgrn2_kernel, L=L, sub=SUBLANES, n_chunks=t_rows // L, layer=layer),
        grid=(bsz, n_heads, nt),
        in_specs=[sec(0), sec(1), sec(2), sec(3),
                  pl.BlockSpec((depth, dk), lambda b, h, t: (0, h)),
                  pl.BlockSpec((1, dk), lambda b, h, t: (0, 0))],
        out_specs=pl.BlockSpec((t_rows, dk), lambda b, h, t: (b * nt + t, h)),
        out_shape=jax.ShapeDtypeStruct((m, d), BF16),
        scratch_shapes=[pltpu.VMEM((dk, dk), F32)],
        compiler_params=_cparams("parallel", "parallel", "arbitrary"),
        name="hgrn2_core",
    )(proj, proj, proj, proj, lb_params.astype(F32), w_gnorm.reshape(1, dk).astype(F32))


def hgrn2_mixer(hn, bsz, w_in, lb_params, layer, w_gnorm, w_out, x_res):
    proj = matmul(hn, w_in.astype(BF16), out_dtype=BF16, tn=1024)
    o = hgrn2_core(proj, bsz, lb_params, layer, w_gnorm)
    return matmul(o, w_out.astype(BF16), residual=x_res)


def conv_ffn(x2d, bsz, g_norm, w_up, conv_w, conv_b, w_down):
    hn = rmsnorm(x2d, g_norm)
    a = ffn_up_conv_act(hn, w_up.astype(BF16), conv_w, conv_b, bsz)
    return matmul(a, w_down.astype(BF16), residual=x2d, tm=512, tk=w_down.shape[0])


def kernel(x, norm_mix, norm_ffn, norm_final, mlstm_w_in, mlstm_b_gates, mlstm_w_norm, mlstm_w_out, diff_w_in, diff_lam_q1, diff_lam_k1, diff_lam_q2, diff_lam_k2, diff_w_subln, diff_w_out, ssm_w_in, ssm_conv_w, ssm_conv_b, ssm_dt_bias, ssm_a_log, ssm_d, ssm_w_norm, ssm_w_out, hgrn_w_in, hgrn_lb, hgrn_w_gnorm, hgrn_w_out, ffn_w_up, ffn_conv_w, ffn_conv_b, ffn_w_down):
    bsz, s, d = x.shape
    depth = norm_mix.shape[0]
    x2d = x.reshape(bsz * s, d)
    for layer in range(depth):
        kind, j = layer % N_MIXERS, layer // N_MIXERS
        hn = rmsnorm(x2d, norm_mix[layer])
        if kind == 0:
            x2d = mlstm_mixer(hn, bsz, mlstm_w_in[j], mlstm_b_gates[j], mlstm_w_norm[j], mlstm_w_out[j], x2d)
        elif kind == 1:
            lambda_init = 0.8 - 0.6 * math.exp(-0.3 * layer)
            x2d = diff_attn_mixer(hn, bsz, diff_w_in[j], diff_lam_q1[j], diff_lam_k1[j], diff_lam_q2[j],
                                  diff_lam_k2[j], diff_w_subln[j], diff_w_out[j], lambda_init, x2d)
        elif kind == 2:
            x2d = mamba2_mixer(hn, bsz, ssm_w_in[j], ssm_conv_w[j], ssm_conv_b[j], ssm_dt_bias[j],
                               ssm_a_log[j], ssm_d[j], ssm_w_norm[j], ssm_w_out[j], x2d)
        else:
            x2d = hgrn2_mixer(hn, bsz, hgrn_w_in[j], hgrn_lb, layer, hgrn_w_gnorm[j], hgrn_w_out[j], x2d)
        x2d = conv_ffn(x2d, bsz, norm_ffn[layer], ffn_w_up[layer], ffn_conv_w[layer], ffn_conv_b[layer],
                       ffn_w_down[layer])
    return rmsnorm(x2d, norm_final, out_dtype=x.dtype).reshape(bsz, s, d)
```
